```python
import math
import jax
import jax.numpy as jnp
from jax import lax
import numpy as np

D_MODEL = 1024
BATCH = 8
SEQ = 2048
DEPTH = 2
DEC_BATCH = 32
DEC_SEQ = 8
PAST_LEN = 8192
PAGE_SIZE = 128

N_A_LAYERS = DEPTH // 2
SSM_GROUP = 16
N_GROUPS = D_MODEL // SSM_GROUP
SSM_STATE = 64
DT_MIN = 1e-3
DT_MAX = 1e-1
N_HEADS = 16
HEAD_DIM = D_MODEL // N_HEADS
N_KV_HEADS = 4
GQA_GROUP = N_HEADS // N_KV_HEADS
CMP_BLOCK = 32
CMP_STRIDE = 16
CMP_HIDDEN = 2 * HEAD_DIM
SEL_BLOCK = 64
N_SEL = 16
WINDOW = 512
WIN_QBLOCK = 128
SEL_QBLOCK = 32
ROPE_THETA = 10000.0
D_FF = 2816
CONV_WIDTH = 3
EPS = 1e-6
NEG = -1e30

kernel_name = 'yoco_s5_nsa_convffn_step'


def rms_norm(x, g):
    x32 = x.astype(jnp.float32)
    y = x32 * lax.rsqrt(jnp.mean(x32 * x32, axis=-1, keepdims=True) + EPS)
    return (y * g.astype(jnp.float32)).astype(x.dtype)


def adaln(c, w, b, n):
    return jnp.split(jax.nn.silu(c) @ w + b, n, axis=-1)


def modulate(h, shift, scale):
    return h * (1.0 + scale[:, None, :]) + shift[:, None, :]


def rope(x, pos):
    half = x.shape[-1] // 2
    freqs = ROPE_THETA ** (-jnp.arange(half, dtype=jnp.float32) / half)
    ang = pos.astype(jnp.float32)[:, None] * freqs[None, :]
    cos = jnp.cos(ang)[None, :, None, :]
    sin = jnp.sin(ang)[None, :, None, :]
    x1, x2 = x[..., :half], x[..., half:]
    return jnp.concatenate([x1 * cos - x2 * sin, x2 * cos + x1 * sin], axis=-1).astype(x.dtype)


def cmul(ar, ai, br, bi):
    return ar * br - ai * bi, ar * bi + ai * br


def s5_mixer(h, lam_re, lam_im, log_dt, b_re, b_im, c_re, c_im, d_skip, w_glu, b_glu, h0):
    f32 = jnp.float32
    bsz, seq, _ = h.shape
    h32 = h.astype(f32)
    u = h32.reshape(bsz, seq, N_GROUPS, SSM_GROUP)
    dt = jnp.exp(log_dt.astype(f32))[:, None]
    lr, li = lam_re.astype(f32), lam_im.astype(f32)
    mag = jnp.exp(lr * dt)
    a_re, a_im = mag * jnp.cos(li * dt), mag * jnp.sin(li * dt)
    den = lr * lr + li * li
    z_re, z_im = cmul(a_re - 1.0, a_im, lr / den, -li / den)
    bu_re, bu_im = cmul(z_re, z_im,
                        jnp.einsum('blgc,gpc->blgp', u, b_re.astype(f32)),
                        jnp.einsum('blgc,gpc->blgp', u, b_im.astype(f32)))
    a_shape = (1, seq) + a_re.shape

    def combine(e1, e2):
        a1r, a1i, b1r, b1i = e1
        a2r, a2i, b2r, b2i = e2
        ar, ai = cmul(a2r, a2i, a1r, a1i)
        br, bi = cmul(a2r, a2i, b1r, b1i)
        return ar, ai, br + b2r, bi + b2i

    acc_re, acc_im, s_re, s_im = lax.associative_scan(
        combine, (jnp.broadcast_to(a_re, a_shape), jnp.broadcast_to(a_im, a_shape), bu_re, bu_im), axis=1)
    if h0 is not None:
        p_re, p_im = cmul(acc_re, acc_im, h0[0].astype(f32)[:, None], h0[1].astype(f32)[:, None])
        s_re, s_im = s_re + p_re, s_im + p_im
    y = (jnp.einsum('blgp,gcp->blgc', s_re, c_re.astype(f32))
         - jnp.einsum('blgp,gcp->blgc', s_im, c_im.astype(f32)))
    y = y.reshape(bsz, seq, D_MODEL) + d_skip.astype(f32) * h32
    g = jax.nn.gelu(y)
    out = g * jax.nn.sigmoid(g @ w_glu.astype(f32) + b_glu.astype(f32))
    return out.astype(h.dtype), s_re[:, -1], s_im[:, -1]


def conv_ffn(h, w_up, conv_w, conv_b, w_down, prefix):
    seq = h.shape[1]
    up = h @ w_up
    ext = jnp.concatenate([prefix.astype(up.dtype), up], axis=1)
    conv = conv_b
    for k in range(CONV_WIDTH):
        conv = conv + conv_w[k] * ext[:, k:k + seq]
    a, g = jnp.split(conv, 2, axis=-1)
    out = (jax.nn.silu(g) * a) @ w_down
    return out, ext[:, -(CONV_WIDTH - 1):]


def shared_kv(x, c, pos, p):
    bsz, seq, _ = x.shape
    shift, scale = adaln(c, p['w_mod_kv'], p['b_mod_kv'], 2)
    h = modulate(rms_norm(x, p['norm_kv']), shift, scale)
    kv = (h @ p['w_kv']).reshape(bsz, seq, 6, N_KV_HEADS, HEAD_DIM)
    rows = jnp.stack([kv[:, :, 0], kv[:, :, 1], rope(kv[:, :, 2], pos), kv[:, :, 3]], axis=2)
    win_rows = jnp.stack([rope(kv[:, :, 4], pos), kv[:, :, 5]], axis=2)
    return rows, win_rows


def compress(rows, pe, w1, b1, w2):
    t = rows.shape[1]
    nc = (t - CMP_BLOCK) // CMP_STRIDE + 1
    idx = np.arange(nc)[:, None] * CMP_STRIDE + np.arange(CMP_BLOCK)[None, :]
    blocks = rows[:, idx] + pe[None, None, :, None, :]
    hid = jax.nn.gelu(jnp.einsum('bnlkd,ldh->bnkh', blocks, w1.reshape(CMP_BLOCK, HEAD_DIM, CMP_HIDDEN)) + b1)
    return jnp.einsum('bnkh,hd->bnkd', hid, w2)


def sel_blocks(k):
    bsz, t = k.shape[:2]
    ns = -(-t // SEL_BLOCK)
    k = jnp.pad(k, ((0, 0), (0, ns * SEL_BLOCK - t), (0, 0), (0, 0)))
    return k.reshape(bsz, ns, SEL_BLOCK, N_KV_HEADS, HEAD_DIM).transpose(0, 3, 1, 2, 4)


def window_band(win_rows):
    seq = win_rows.shape[1]
    nqb = seq // WIN_QBLOCK
    pad = jnp.pad(win_rows, ((0, 0), (WINDOW, 0), (0, 0), (0, 0), (0, 0)))
    idx = np.arange(nqb)[:, None] * WIN_QBLOCK + np.arange(WINDOW + WIN_QBLOCK)[None, :]
    band = pad[:, idx]
    return band[:, :, :, 0], band[:, :, :, 1], jnp.asarray(idx - WINDOW)


def cmp_to_sel_matrix(nc, ns):
    r = SEL_BLOCK // CMP_STRIDE
    q = CMP_BLOCK // CMP_STRIDE
    offs = (np.arange(r)[:, None] - np.arange(q)[None, :]).reshape(-1)
    target = r * np.arange(ns)[None, :, None] + offs[None, None, :]
    m = (np.arange(nc)[:, None, None] == target).sum(-1)
    return jnp.asarray(m, jnp.float32)


def compressed_attention(qg, kc, vc, q_pos):
    nc = kc.shape[1]
    s = jnp.einsum('bqkgd,bckd->bkgqc', qg, kc).astype(jnp.float32)
    ends = jnp.arange(nc) * CMP_STRIDE + CMP_BLOCK - 1
    valid = ends[None, :] <= q_pos[:, None]
    p = jax.nn.softmax(jnp.where(valid, s, NEG), axis=-1) * valid
    o = jnp.einsum('bkgqc,bckd->bqkgd', p.astype(vc.dtype), vc)
    return o, p


def select_blocks(p_cmp, q_pos, ns):
    nc = p_cmp.shape[-1]
    score = jnp.einsum('bkgqc,cs->bkqs', p_cmp, cmp_to_sel_matrix(nc, ns))
    j = jnp.arange(ns)[None, :]
    cur = q_pos[:, None] // SEL_BLOCK
    valid = j * SEL_BLOCK <= q_pos[:, None]
    forced = (j == 0) | (j == cur) | (j == cur - 1)
    score = jnp.where(forced, jnp.inf, jnp.where(valid, score, -jnp.inf))
    return lax.top_k(score, min(N_SEL, ns))[1]


def selected_attention(qg, kb, vb, idx, q_pos):
    bsz, nkv = kb.shape[:2]
    bi = jnp.arange(bsz)[:, None, None, None]
    hi = jnp.arange(nkv)[None, :, None, None]
    kg = kb[bi, hi, idx]
    vg = vb[bi, hi, idx]
    s = jnp.einsum('bqkgd,bkqsld->bkgqsl', qg, kg).astype(jnp.float32)
    kpos = idx[..., None] * SEL_BLOCK + jnp.arange(SEL_BLOCK)
    mask = (kpos <= q_pos[None, None, :, None, None])[:, :, None]
    sh = s.shape
    p = jax.nn.softmax(jnp.where(mask, s, NEG).reshape(sh[:4] + (-1,)), axis=-1).reshape(sh)
    return jnp.einsum('bkgqsl,bkqsld->bqkgd', p.astype(vg.dtype), vg)


def selected_attention_blocked(qg, kb, vb, idx, q_pos, qblock):
    bsz, nq = qg.shape[:2]
    nb = nq // qblock
    qs = jnp.moveaxis(qg.reshape((bsz, nb, qblock) + qg.shape[2:]), 1, 0)
    ids = jnp.moveaxis(idx.reshape(idx.shape[:2] + (nb, qblock, idx.shape[-1])), 2, 0)
    ps = q_pos.reshape(nb, qblock)
    out = lax.map(lambda a: selected_attention(a[0], kb, vb, a[1], a[2]), (qs, ids, ps))
    return jnp.moveaxis(out, 0, 1).reshape((bsz, nq) + out.shape[3:])


def window_attention(qg, kb, vb, q_pos, k_pos):
    bsz, nq = qg.shape[:2]
    n = kb.shape[1]
    qb = qg.reshape((bsz, n, nq // n) + qg.shape[2:])
    qp = q_pos.reshape(n, nq // n)
    s = jnp.einsum('bnqkgd,bnskd->bnkgqs', qb, kb).astype(jnp.float32)
    diff = qp[:, :, None] - k_pos[:, None, :]
    mask = (diff >= 0) & (diff < WINDOW) & (k_pos[:, None, :] >= 0)
    p = jax.nn.softmax(jnp.where(mask[None, :, None, None], s, NEG), axis=-1)
    o = jnp.einsum('bnkgqs,bnskd->bnqkgd', p.astype(vb.dtype), vb)
    return o.reshape((bsz, nq) + o.shape[3:])


def nsa_mixer(h, pos, w_qg, w_o, ctx):
    bsz, nq, _ = h.shape
    proj = h @ w_qg
    q = proj[..., :N_HEADS * HEAD_DIM].reshape(bsz, nq, N_HEADS, HEAD_DIM)
    gates = jax.nn.sigmoid(proj[..., N_HEADS * HEAD_DIM:].astype(jnp.float32)).reshape(bsz, nq, 3, N_HEADS, 1)
    scale = HEAD_DIM ** -0.5
    grp = (bsz, nq, N_KV_HEADS, GQA_GROUP, HEAD_DIM)
    q_plain = (q * scale).reshape(grp)
    q_rot = (rope(q, pos) * scale).reshape(grp)
    o_cmp, p_cmp = compressed_attention(q_plain, ctx['kc'], ctx['vc'], pos)
    idx = select_blocks(p_cmp, pos, ctx['ks'].shape[2])
    qblk = SEL_QBLOCK if nq % SEL_QBLOCK == 0 else nq
    o_sel = selected_attention_blocked(q_rot, ctx['ks'], ctx['vs'], idx, pos, qblk)
    o_win = window_attention(q_rot, ctx['wk'], ctx['wv'], pos, ctx['wpos'])
    hs = (bsz, nq, N_HEADS, HEAD_DIM)
    o = (gates[:, :, 0] * o_cmp.reshape(hs) + gates[:, :, 1] * o_sel.reshape(hs)
         + gates[:, :, 2] * o_win.reshape(hs))
    return o.reshape(bsz, nq, N_HEADS * HEAD_DIM).astype(h.dtype) @ w_o


def build_context(rows, win_rows, past, p):
    seq = rows.shape[1]
    if past is None:
        full = rows
        wk, wv, wpos = window_band(win_rows)
        win_state = win_rows[:, -min(WINDOW, seq):]
    else:
        pool, table = past['pool'], past['page_table']
        past_rows = pool[table].reshape((table.shape[0], table.shape[1] * pool.shape[1]) + pool.shape[2:])
        past_len = past_rows.shape[1]
        full = jnp.concatenate([past_rows.astype(rows.dtype), rows], axis=1)
        w_buf = past['win'].shape[1]
        win_all = jnp.concatenate([past['win'].astype(win_rows.dtype), win_rows], axis=1)[:, None]
        wk, wv = win_all[:, :, :, 0], win_all[:, :, :, 1]
        wpos = (past_len - w_buf + jnp.arange(w_buf + seq))[None, :]
        win_state = win_rows
    kc = compress(full[:, :, 0], p['cmp_pe'][0], p['cmp_w1'][0], p['cmp_b1'][0], p['cmp_w2'][0])
    vc = compress(full[:, :, 1], p['cmp_pe'][1], p['cmp_w1'][1], p['cmp_b1'][1], p['cmp_w2'][1])
    ctx = {'kc': kc, 'vc': vc, 'ks': sel_blocks(full[:, :, 2]), 'vs': sel_blocks(full[:, :, 3]),
           'wk': wk, 'wv': wv, 'wpos': wpos}
    return ctx, win_state


def group_forward(x, c, pos, p, past):
    bsz = x.shape[0]
    ssm_re, ssm_im, convs = [], [], []
    ctx = None
    for layer in range(DEPTH):
        shift, scale, gate = adaln(c, p['w_mod'][layer, 0], p['b_mod'][layer, 0], 3)
        h = modulate(rms_norm(x, p['norm_pre'][layer, 0]), shift, scale)
        if layer < N_A_LAYERS:
            h0 = None if past is None else (past['ssm_re'][layer], past['ssm_im'][layer])
            out, s_re, s_im = s5_mixer(h, p['ssm_lam_re'][layer], p['ssm_lam_im'][layer], p['ssm_log_dt'][layer],
                                       p['ssm_b_re'][layer], p['ssm_b_im'][layer], p['ssm_c_re'][layer],
                                       p['ssm_c_im'][layer], p['ssm_d'][layer], p['w_glu'][layer],
                                       p['b_glu'][layer], h0)
            ssm_re.append(s_re)
            ssm_im.append(s_im)
        else:
            j = layer - N_A_LAYERS
            out = nsa_mixer(h, pos, p['w_qg'][j], p['w_o'][j], ctx)
        x = x + gate[:, None, :] * rms_norm(out, p['norm_post'][layer, 0])
        shift, scale, gate = adaln(c, p['w_mod'][layer, 1], p['b_mod'][layer, 1], 3)
        h = modulate(rms_norm(x, p['norm_pre'][layer, 1]), shift, scale)
        if past is None:
            prefix = jnp.zeros((bsz, CONV_WIDTH - 1, 2 * D_FF), h.dtype)
        else:
            prefix = past['conv'][layer]
        out, conv_state = conv_ffn(h, p['ffn_w_up'][layer], p['ffn_conv_w'][layer], p['ffn_conv_b'][layer],
                                   p['ffn_w_down'][layer], prefix)
        convs.append(conv_state)
        x = x + gate[:, None, :] * rms_norm(out, p['norm_post'][layer, 1])
        if layer == N_A_LAYERS - 1:
            rows, win_rows = shared_kv(x, c, pos, p)
            ctx, win_state = build_context(rows, win_rows, past, p)
    return x, rows, win_state, jnp.stack(ssm_re), jnp.stack(ssm_im), jnp.stack(convs)


def setup_inputs(seed: int = 0) -> dict:
    key = jax.random.key(seed)
    ks = iter(jax.random.split(key, 64))

    def nrm(shape, scale=1.0):
        return jax.random.normal(next(ks), shape, jnp.float32) * scale

    n_a, n_b = N_A_LAYERS, DEPTH - N_A_LAYERS
    n_pages = PAST_LEN // PAGE_SIZE
    n_used = DEC_BATCH * n_pages
    n_phys = n_used + max(1, n_used // 4)
    kvd = (N_KV_HEADS, HEAD_DIM)
    d2f = 2 * D_FF
    page_table = jax.random.permutation(next(ks), n_phys)[:n_used].reshape(DEC_BATCH, n_pages).astype(jnp.int32)
    lam_im = jnp.pi * jnp.arange(SSM_STATE, dtype=jnp.float32) + nrm((n_a, N_GROUPS, SSM_STATE), 0.01)
    log_dt = jax.random.uniform(next(ks), (n_a, N_GROUPS), jnp.float32, math.log(DT_MIN), math.log(DT_MAX))
    return {
        'x_prompt': nrm((BATCH, SEQ, D_MODEL)),
        'x_sample': nrm((DEC_BATCH, DEC_SEQ, D_MODEL)),
        'cache_nsa_paged': nrm((n_phys, PAGE_SIZE, 4) + kvd),
        'cache_nsa_window': nrm((DEC_BATCH, min(WINDOW, PAST_LEN), 2) + kvd),
        'state_ssm_re': nrm((n_a, DEC_BATCH, N_GROUPS, SSM_STATE), 0.1),
        'state_ssm_im': nrm((n_a, DEC_BATCH, N_GROUPS, SSM_STATE), 0.1),
        'state_ffn_conv': nrm((DEPTH, DEC_BATCH, CONV_WIDTH - 1, d2f)),
        'page_table': page_table,
        'c_prompt': nrm((BATCH, D_MODEL)),
        'c_sample': nrm((DEC_BATCH, D_MODEL)),
        'norm_pre': 1.0 + nrm((DEPTH, 2, D_MODEL), 0.01),
        'norm_post': 1.0 + nrm((DEPTH, 2, D_MODEL), 0.01),
        'w_mod': nrm((DEPTH, 2, D_MODEL, 3 * D_MODEL), 0.5 * D_MODEL ** -0.5),
        'b_mod': nrm((DEPTH, 2, 3 * D_MODEL), 0.2),
        'ssm_lam_re': -0.5 + nrm((n_a, N_GROUPS, SSM_STATE), 0.01),
        'ssm_lam_im': lam_im,
        'ssm_log_dt': log_dt,
        'ssm_b_re': nrm((n_a, N_GROUPS, SSM_STATE, SSM_GROUP), (2 * SSM_GROUP) ** -0.5),
        'ssm_b_im': nrm((n_a, N_GROUPS, SSM_STATE, SSM_GROUP), (2 * SSM_GROUP) ** -0.5),
        'ssm_c_re': nrm((n_a, N_GROUPS, SSM_GROUP, SSM_STATE), (2 * SSM_STATE) ** -0.5),
        'ssm_c_im': nrm((n_a, N_GROUPS, SSM_GROUP, SSM_STATE), (2 * SSM_STATE) ** -0.5),
        'ssm_d': nrm((n_a, D_MODEL), 0.5),
        'w_glu': nrm((n_a, D_MODEL, D_MODEL), D_MODEL ** -0.5),
        'b_glu': nrm((n_a, D_MODEL), 0.02),
        'ffn_w_up': nrm((DEPTH, D_MODEL, d2f), D_MODEL ** -0.5),
        'ffn_conv_w': nrm((DEPTH, CONV_WIDTH, d2f), CONV_WIDTH ** -0.5),
        'ffn_conv_b': nrm((DEPTH, d2f), 0.02),
        'ffn_w_down': nrm((DEPTH, D_FF, D_MODEL), D_FF ** -0.5),
        'norm_kv': 1.0 + nrm((D_MODEL,), 0.01),
        'w_mod_kv': nrm((D_MODEL, 2 * D_MODEL), 0.5 * D_MODEL ** -0.5),
        'b_mod_kv': nrm((2 * D_MODEL,), 0.2),
        'w_kv': nrm((D_MODEL, 6 * N_KV_HEADS * HEAD_DIM), D_MODEL ** -0.5),
        'cmp_pe': nrm((2, CMP_BLOCK, HEAD_DIM), 0.5),
        'cmp_w1': nrm((2, CMP_BLOCK * HEAD_DIM, CMP_HIDDEN), (CMP_BLOCK * HEAD_DIM) ** -0.5),
        'cmp_b1': nrm((2, CMP_HIDDEN), 0.02),
        'cmp_w2': nrm((2, CMP_HIDDEN, HEAD_DIM), CMP_HIDDEN ** -0.5),
        'w_qg': nrm((n_b, D_MODEL, N_HEADS * HEAD_DIM + 3 * N_HEADS), D_MODEL ** -0.5),
        'w_o': nrm((n_b, N_HEADS * HEAD_DIM, D_MODEL), (N_HEADS * HEAD_DIM) ** -0.5),
    }


def reference(x_prompt, x_sample, cache_nsa_paged, cache_nsa_window, state_ssm_re, state_ssm_im,
              state_ffn_conv, page_table, c_prompt, c_sample, norm_pre, norm_post, w_mod, b_mod,
              ssm_lam_re, ssm_lam_im, ssm_log_dt, ssm_b_re, ssm_b_im, ssm_c_re, ssm_c_im, ssm_d,
              w_glu, b_glu, ffn_w_up, ffn_conv_w, ffn_conv_b, ffn_w_down, norm_kv, w_mod_kv, b_mod_kv,
              w_kv, cmp_pe, cmp_w1, cmp_b1, cmp_w2, w_qg, w_o):
    p = {'norm_pre': norm_pre, 'norm_post': norm_post, 'w_mod': w_mod, 'b_mod': b_mod,
         'ssm_lam_re': ssm_lam_re, 'ssm_lam_im': ssm_lam_im, 'ssm_log_dt': ssm_log_dt,
         'ssm_b_re': ssm_b_re, 'ssm_b_im': ssm_b_im, 'ssm_c_re': ssm_c_re, 'ssm_c_im': ssm_c_im,
         'ssm_d': ssm_d, 'w_glu': w_glu, 'b_glu': b_glu, 'ffn_w_up': ffn_w_up, 'ffn_conv_w': ffn_conv_w,
         'ffn_conv_b': ffn_conv_b, 'ffn_w_down': ffn_w_down, 'norm_kv': norm_kv, 'w_mod_kv': w_mod_kv,
         'b_mod_kv': b_mod_kv, 'w_kv': w_kv, 'cmp_pe': cmp_pe, 'cmp_w1': cmp_w1, 'cmp_b1': cmp_b1,
         'cmp_w2': cmp_w2, 'w_qg': w_qg, 'w_o': w_o}
    past = {'pool': cache_nsa_paged, 'page_table': page_table, 'win': cache_nsa_window,
            'ssm_re': state_ssm_re, 'ssm_im': state_ssm_im, 'conv': state_ffn_conv}
    past_len = page_table.shape[1] * cache_nsa_paged.shape[1]
    pos_p = jnp.arange(x_prompt.shape[1])
    pos_s = past_len + jnp.arange(x_sample.shape[1])
    y_p, rows_p, win_p, sre_p, sim_p, conv_p = group_forward(x_prompt, c_prompt, pos_p, p, None)
    y_s, rows_s, win_s, sre_s, sim_s, conv_s = group_forward(x_sample, c_sample, pos_s, p, past)
    return (y_p, y_s, rows_p, rows_s, win_p, win_s, sre_p, sim_p, sre_s, sim_s, conv_p, conv_s)
```

```python
import functools
import math

import numpy as np
import jax
import jax.numpy as jnp
from jax import lax
from jax.experimental import pallas as pl
from jax.experimental.pallas import tpu as pltpu

D_MODEL = 1024
SSM_GROUP = 16
N_GROUPS = D_MODEL // SSM_GROUP
SSM_STATE = 64
N_STATE = N_GROUPS * SSM_STATE
N_HEADS = 16
HEAD_DIM = 64
N_KV_HEADS = 4
GQA_GROUP = N_HEADS // N_KV_HEADS
KV_WIDTH = N_KV_HEADS * HEAD_DIM
CMP_BLOCK = 32
CMP_STRIDE = 16
CMP_HIDDEN = 2 * HEAD_DIM
SEL_BLOCK = 64
N_SEL = 16
WINDOW = 512
ROPE_THETA = 10000.0
D_FF = 2816
CONV_WIDTH = 3
EPS = 1e-6
NEG = -1e30

SUBLANES = 8
LANES = 128
MXU_DIM = 256
VMEM_LIMIT = 56 * 1024 * 1024

F32 = jnp.float32
BF16 = jnp.bfloat16


def _cparams(*sem):
    return pltpu.CompilerParams(dimension_semantics=sem, vmem_limit_bytes=VMEM_LIMIT)


def _sigmoid(x):
    return 1.0 / (1.0 + jnp.exp(-x))


def _silu(x):
    return x * _sigmoid(x)


def _gelu(x):
    return 0.5 * x * (1.0 + jnp.tanh(math.sqrt(2.0 / math.pi) * (x + 0.044715 * (x * x * x))))


def _rms(x, g):
    return x * lax.rsqrt(jnp.mean(x * x, axis=-1, keepdims=True) + EPS) * g


def _dot(a, b):
    return jnp.dot(a, b, preferred_element_type=F32)


def _dot_nt(a, b):
    return lax.dot_general(a, b, (((1,), (1,)), ((), ())), preferred_element_type=F32)


def _mods_kernel(c_ref, w_ref, b_ref, o_ref):
    c = _silu(c_ref[...])
    o_ref[0] = jnp.dot(c, w_ref[0], preferred_element_type=F32,
                       precision=lax.Precision.HIGHEST) + b_ref[0]


def _mods(c, w, b):
    s, d, n = w.shape
    m = c.shape[0]
    tn = 1024
    return pl.pallas_call(
        _mods_kernel,
        grid=(s, n // tn),
        in_specs=[pl.BlockSpec((m, d), lambda i, j: (0, 0)),
                  pl.BlockSpec((1, d, tn), lambda i, j: (i, 0, j)),
                  pl.BlockSpec((1, 1, tn), lambda i, j: (i, 0, j))],
        out_specs=pl.BlockSpec((1, m, tn), lambda i, j: (i, 0, j)),
        out_shape=jax.ShapeDtypeStruct((s, m, n), F32),
        compiler_params=_cparams("parallel", "parallel"),
        name="adaln_mods",
    )(c, w, b.reshape(s, 1, n))


def _s5_prep_kernel(lr_ref, li_ref, ldt_ref, bre_ref, bim_ref, are_ref, aim_ref, zbre_ref, zbim_ref):
    lr, li = lr_ref[...], li_ref[...]
    dt = jnp.exp(ldt_ref[...])
    mag = jnp.exp(lr * dt)
    a_re, a_im = mag * jnp.cos(li * dt), mag * jnp.sin(li * dt)
    den = lr * lr + li * li
    ir, ii = lr / den, -li / den
    z_re = (a_re - 1.0) * ir - a_im * ii
    z_im = (a_re - 1.0) * ii + a_im * ir
    are_ref[...] = a_re
    aim_ref[...] = a_im
    bre, bim = bre_ref[...], bim_ref[...]
    zr, zi = z_re[:, None, :], z_im[:, None, :]
    zbre_ref[...] = zr * bre - zi * bim
    zbim_ref[...] = zr * bim + zi * bre


def _s5_prep(lam_re, lam_im, log_dt, b_re, b_im):
    g, p, c = b_re.shape
    return pl.pallas_call(
        _s5_prep_kernel,
        out_shape=(jax.ShapeDtypeStruct((g, p), F32), jax.ShapeDtypeStruct((g, p), F32),
                   jax.ShapeDtypeStruct((g, c, p), F32), jax.ShapeDtypeStruct((g, c, p), F32)),
        name="s5_prep",
    )(lam_re, lam_im, log_dt.reshape(g, 1), b_re.transpose(0, 2, 1), b_im.transpose(0, 2, 1))


S5_KBLOCKS = D_MODEL // MXU_DIM
S5_GPB = MXU_DIM // SSM_GROUP
S5_SPB = S5_GPB * SSM_STATE


def _s5_block_diag_in(zb):
    z = zb.reshape(S5_KBLOCKS, S5_GPB, SSM_GROUP, SSM_STATE)
    eye = jnp.eye(S5_GPB, dtype=zb.dtype)
    return jnp.einsum('kgcp,gh->kgchp', z, eye).reshape(S5_KBLOCKS, MXU_DIM, S5_SPB)


def _s5_block_diag_out(cm):
    z = cm.reshape(S5_KBLOCKS, S5_GPB, SSM_GROUP, SSM_STATE)
    eye = jnp.eye(S5_GPB, dtype=cm.dtype)
    return jnp.einsum('kgcp,gh->kgphc', z, eye).reshape(S5_KBLOCKS, S5_SPB, MXU_DIM)


S5_LANE_BLOCKS = 4


def _to_time_major(dst, b, val, t, bt):
    for c in range(val.shape[1] // LANES):
        dst[c, pl.ds(b, t, stride=bt), :] = val[:, c * LANES:(c + 1) * LANES]


def _from_time_major(src, b, t, bt):
    return jnp.concatenate([src[c, pl.ds(b, t, stride=bt), :] for c in range(src.shape[0])], axis=1)


def _slab_cols(src, c0, c1):
    return jnp.concatenate([src[c] for c in range(c0 // LANES, c1 // LANES)], axis=1)


def _s5_kernel(x_ref, mod_ref, npre_ref, npost_ref, wb_ref, wcre_ref, wcim_ref, a_ref, d_ref,
               wglu_ref, bglu_ref, h0_ref, y_ref, sout_ref, htm, bu, ytm, otm, state, *, bt, t):
    nsub = bt // SUBLANES

    @pl.when(pl.program_id(0) == 0)
    def _():
        state[...] = h0_ref[...]

    for b in range(bt):
        hb = _rms(x_ref[b], npre_ref[...]) * (1.0 + mod_ref[1, b:b + 1, :]) + mod_ref[0, b:b + 1, :]
        _to_time_major(htm, b, hb, t, bt)

    for kb in range(S5_KBLOCKS):
        cs = slice(kb * MXU_DIM, (kb + 1) * MXU_DIM)
        hkb = _slab_cols(htm, kb * MXU_DIM, (kb + 1) * MXU_DIM)
        bu[...] = _dot(hkb.astype(BF16), wb_ref[kb])
        for lb0 in range(0, S5_SPB // LANES, S5_LANE_BLOCKS):
            for sb in range(nsub):
                cols = [(lb0 + i) * LANES for i in range(S5_LANE_BLOCKS)]
                rs = slice(sb * SUBLANES, (sb + 1) * SUBLANES)
                ar = [jnp.broadcast_to(a_ref[0:1, kb * S5_SPB + c:kb * S5_SPB + c + LANES], (SUBLANES, LANES))
                      for c in cols]
                ai = [jnp.broadcast_to(a_ref[1:2, kb * S5_SPB + c:kb * S5_SPB + c + LANES], (SUBLANES, LANES))
                      for c in cols]
                init = tuple(state[0, rs, kb * S5_SPB + c:kb * S5_SPB + c + LANES] for c in cols) + \
                    tuple(state[1, rs, kb * S5_SPB + c:kb * S5_SPB + c + LANES] for c in cols)

                def step(i, carry, cols=cols, ar=ar, ai=ai, sb=sb):
                    r0 = pl.multiple_of(i * bt + sb * SUBLANES, SUBLANES)
                    out = [None] * (2 * S5_LANE_BLOCKS)
                    for n, c in enumerate(cols):
                        sr, si = carry[n], carry[S5_LANE_BLOCKS + n]
                        nr = ar[n] * sr - ai[n] * si + bu[pl.ds(r0, SUBLANES), c:c + LANES]
                        ni = ar[n] * si + ai[n] * sr + bu[pl.ds(r0, SUBLANES), S5_SPB + c:S5_SPB + c + LANES]
                        bu[pl.ds(r0, SUBLANES), c:c + LANES] = nr
                        bu[pl.ds(r0, SUBLANES), S5_SPB + c:S5_SPB + c + LANES] = ni
                        out[n], out[S5_LANE_BLOCKS + n] = nr, ni
                    return tuple(out)

                fin = lax.fori_loop(0, t, step, init, unroll=min(t, 8))
                for n, c in enumerate(cols):
                    state[0, rs, kb * S5_SPB + c:kb * S5_SPB + c + LANES] = fin[n]
                    state[1, rs, kb * S5_SPB + c:kb * S5_SPB + c + LANES] = fin[S5_LANE_BLOCKS + n]
        ykb = _dot(bu[:, :S5_SPB].astype(BF16), wcre_ref[kb]) - _dot(bu[:, S5_SPB:].astype(BF16), wcim_ref[kb])
        ytm[:, cs] = ykb + d_ref[:, cs] * hkb

    g = _gelu(ytm[...])
    out = g * _sigmoid(_dot(g.astype(BF16), wglu_ref[...]) + bglu_ref[...])
    out = _rms(out, npost_ref[...])
    for c in range(otm.shape[0]):
        otm[c] = out[:, c * LANES:(c + 1) * LANES]
    for b in range(bt):
        y_ref[b] = x_ref[b] + mod_ref[2, b:b + 1, :] * _from_time_major(otm, b, t, bt)
    sout_ref[...] = state[...]


def _s5_layer(x, mod, npre, npost, wb, wcre, wcim, a, dskip, wglu, bglu, h0, t):
    bsz, seq, d = x.shape
    t = min(t, seq)
    rows = bsz * t
    kern = functools.partial(_s5_kernel, bt=bsz, t=t)
    full = lambda shape: pl.BlockSpec(shape, lambda i: (0,) * len(shape))
    return pl.pallas_call(
        kern,
        grid=(seq // t,),
        in_specs=[pl.BlockSpec((bsz, t, d), lambda i: (0, i, 0)),
                  full((3, bsz, d)), full((1, d)), full((1, d)),
                  full(wb.shape), full(wcre.shape), full(wcim.shape), full((2, N_STATE)), full((1, d)),
                  full((d, d)), full((1, d)), full((2, bsz, N_STATE))],
        out_specs=(pl.BlockSpec((bsz, t, d), lambda i: (0, i, 0)), full((2, bsz, N_STATE))),
        out_shape=(jax.ShapeDtypeStruct((bsz, seq, d), F32), jax.ShapeDtypeStruct((2, bsz, N_STATE), F32)),
        scratch_shapes=[pltpu.VMEM((d // LANES, rows, LANES), F32), pltpu.VMEM((rows, 2 * S5_SPB), F32),
                        pltpu.VMEM((rows, d), F32), pltpu.VMEM((d // LANES, rows, LANES), F32),
                        pltpu.VMEM((2, bsz, N_STATE), F32)],
        compiler_params=_cparams("arbitrary"),
        name="s5_layer",
    )(x, mod, npre, npost, wb, wcre, wcim, a, dskip, wglu, bglu, h0)


def _ffn_kernel(x_ref, mod_ref, npre_ref, npost_ref, wa_ref, wg_ref, cwa_ref, cwg_ref, cba_ref, cbg_ref,
                wd_ref, pfa_ref, pfg_ref, y_ref, sta_ref, stg_ref, htm, hbf, ea, eg, acc, cara, carg, *, bt, t):
    i, j = pl.program_id(0), pl.program_id(1)
    rows = bt * t
    halo = (CONV_WIDTH - 1) * bt

    @pl.when(j == 0)
    def _():
        for b in range(bt):
            hb = _rms(x_ref[b], npre_ref[...]) * (1.0 + mod_ref[1, b:b + 1, :]) + mod_ref[0, b:b + 1, :]
            _to_time_major(htm, b, hb, t, bt)
        hbf[...] = _slab_cols(htm, 0, D_MODEL).astype(BF16)
        acc[...] = jnp.zeros_like(acc)

    @pl.when(i == 0)
    def _():
        cara[j] = pfa_ref[...]
        carg[j] = pfg_ref[...]

    h = hbf[...]

    def conv(e, w_ref, car, cw_ref, cb_ref):
        e[0:halo, :] = car[j]
        e[halo:halo + rows, :] = _dot(h, w_ref[...])
        out = cb_ref[...] + cw_ref[0:1, :] * e[0:rows, :]
        for k in range(1, CONV_WIDTH):
            out = out + cw_ref[k:k + 1, :] * e[k * bt:k * bt + rows, :]
        car[j] = e[rows:rows + halo, :]
        return out

    a = conv(ea, wa_ref, cara, cwa_ref, cba_ref)
    g = conv(eg, wg_ref, carg, cwg_ref, cbg_ref)
    acc[...] += _dot((_silu(g) * a).astype(BF16), wd_ref[...])
    sta_ref[j] = cara[j]
    stg_ref[j] = carg[j]

    @pl.when(j == pl.num_programs(1) - 1)
    def _():
        out = _rms(acc[...], npost_ref[...])
        for c in range(htm.shape[0]):
            htm[c] = out[:, c * LANES:(c + 1) * LANES]
        for b in range(bt):
            y_ref[b] = x_ref[b] + mod_ref[2, b:b + 1, :] * _from_time_major(htm, b, t, bt)


def _ffn_layer(x, mod, npre, npost, wa, wg, cwa, cwg, cba, cbg, wd, pfa, pfg, t, tn):
    bsz, seq, d = x.shape
    t = min(t, seq)
    rows = bsz * t
    halo = (CONV_WIDTH - 1) * bsz
    nj = D_FF // tn
    kern = functools.partial(_ffn_kernel, bt=bsz, t=t)
    full = lambda shape: pl.BlockSpec(shape, lambda i, j: (0,) * len(shape))
    col = lambda r: pl.BlockSpec((r, tn), lambda i, j: (0, j))
    return pl.pallas_call(
        kern,
        grid=(seq // t, nj),
        in_specs=[pl.BlockSpec((bsz, t, d), lambda i, j: (0, i, 0)),
                  full((3, bsz, d)), full((1, d)), full((1, d)),
                  col(d), col(d), col(CONV_WIDTH), col(CONV_WIDTH), col(1), col(1),
                  pl.BlockSpec((tn, d), lambda i, j: (j, 0)), col(halo), col(halo)],
        out_specs=(pl.BlockSpec((bsz, t, d), lambda i, j: (0, i, 0)), full((nj, halo, tn)), full((nj, halo, tn))),
        out_shape=(jax.ShapeDtypeStruct((bsz, seq, d), F32), jax.ShapeDtypeStruct((nj, halo, tn), F32),
                   jax.ShapeDtypeStruct((nj, halo, tn), F32)),
        scratch_shapes=[pltpu.VMEM((d // LANES, rows, LANES), F32), pltpu.VMEM((rows, d), BF16),
                        pltpu.VMEM((rows + halo, tn), F32),
                        pltpu.VMEM((rows + halo, tn), F32), pltpu.VMEM((rows, d), F32),
                        pltpu.VMEM((nj, halo, tn), F32), pltpu.VMEM((nj, halo, tn), F32)],
        compiler_params=_cparams("arbitrary", "arbitrary"),
        name="conv_ffn",
    )(x, mod, npre, npost, wa, wg, cwa, cwg, cba, cbg, wd, pfa, pfg)


def _rope_tables(pos):
    half = HEAD_DIM // 2
    freqs = (ROPE_THETA ** (-np.arange(half, dtype=np.float32) / half)).astype(np.float32)
    ang = (pos.astype(np.float32)[:, None] * freqs[None, :]).astype(np.float32).astype(np.float64)
    cos = np.concatenate([np.cos(ang), np.cos(ang)], axis=1)
    sin = np.concatenate([-np.sin(ang), np.sin(ang)], axis=1)
    rep = LANES // HEAD_DIM
    return (jnp.asarray(np.tile(cos, (1, rep)), F32), jnp.asarray(np.tile(sin, (1, rep)), F32))


def _rope(x, cos, sin):
    half = HEAD_DIM // 2
    first = (lax.broadcasted_iota(jnp.int32, cos.shape, 1) % HEAD_DIM) < half
    out = []
    for c in range(x.shape[1] // LANES):
        s = x[:, c * LANES:(c + 1) * LANES]
        swapped = jnp.where(first, pltpu.roll(s, LANES - half, 1), pltpu.roll(s, half, 1))
        out.append(s * cos + swapped * sin)
    return jnp.concatenate(out, axis=1)


QG_PAD = N_HEADS * HEAD_DIM + LANES


def _proj_kernel(x_ref, mkv_ref, mq_ref, nkv_ref, nq_ref, wkv_ref, wqg_ref, cos_ref, sin_ref,
                 rows_ref, win_ref, kvb_ref, qp_ref, qr_ref, gate_ref, hkv, hq, *, bt, t):
    for b in range(bt):
        xb = x_ref[b]
        xn = xb * lax.rsqrt(jnp.mean(xb * xb, axis=-1, keepdims=True) + EPS)
        hkv[b * t:(b + 1) * t, :] = (xn * nkv_ref[...]) * (1.0 + mkv_ref[1, b]) + mkv_ref[0, b]
        hq[b * t:(b + 1) * t, :] = (xn * nq_ref[...]) * (1.0 + mq_ref[1, b]) + mq_ref[0, b]
    cos, sin = cos_ref[...], sin_ref[...]
    kv = _dot(hkv[...].astype(BF16), wkv_ref[...])
    w = KV_WIDTH
    ksel = _rope(kv[:, 2 * w:3 * w], cos, sin)
    kwin = _rope(kv[:, 4 * w:5 * w], cos, sin)
    rows = jnp.concatenate([kv[:, :2 * w], ksel, kv[:, 3 * w:4 * w]], axis=1)
    win = jnp.concatenate([kwin, kv[:, 5 * w:]], axis=1)
    qg = _dot(hq[...].astype(BF16), wqg_ref[...])
    q = qg[:, :N_HEADS * HEAD_DIM] * (HEAD_DIM ** -0.5)
    qrot = _rope(q, cos, sin)
    gates = _sigmoid(qg[:, N_HEADS * HEAD_DIM:])
    for b in range(bt):
        rs = slice(b * t, (b + 1) * t)
        rows_ref[b] = rows[rs]
        win_ref[b] = win[rs]
        kvb_ref[b] = jnp.concatenate([rows[rs, 2 * w:], win[rs]], axis=1).astype(BF16)
        qp_ref[b] = q[rs].astype(BF16)
        qr_ref[b] = qrot[rs].astype(BF16)
        gate_ref[b] = gates[rs]


def _proj(x, mkv, mq, nkv, nq, wkv, wqg, cos, sin, bt, t):
    bsz, seq, d = x.shape
    t = min(t, seq)
    r = bt * t
    kern = functools.partial(_proj_kernel, bt=bt, t=t)
    full = lambda shape: pl.BlockSpec(shape, lambda b, i: (0,) * len(shape))
    tile = lambda n: pl.BlockSpec((bt, t, n), lambda b, i: (b, i, 0))
    mod = lambda n: pl.BlockSpec((n, bt, 1, d), lambda b, i: (0, b, 0, 0))
    tab = pl.BlockSpec((r, LANES), lambda b, i: (i, 0))
    nq_cols = N_HEADS * HEAD_DIM
    return pl.pallas_call(
        kern,
        grid=(bsz // bt, seq // t),
        in_specs=[tile(d), mod(2), mod(3), full((1, d)), full((1, d)), full(wkv.shape), full(wqg.shape),
                  tab, tab],
        out_specs=(tile(4 * KV_WIDTH), tile(2 * KV_WIDTH), tile(4 * KV_WIDTH), tile(nq_cols), tile(nq_cols),
                   tile(LANES)),
        out_shape=(jax.ShapeDtypeStruct((bsz, seq, 4 * KV_WIDTH), F32),
                   jax.ShapeDtypeStruct((bsz, seq, 2 * KV_WIDTH), F32),
                   jax.ShapeDtypeStruct((bsz, seq, 4 * KV_WIDTH), BF16),
                   jax.ShapeDtypeStruct((bsz, seq, nq_cols), BF16),
                   jax.ShapeDtypeStruct((bsz, seq, nq_cols), BF16),
                   jax.ShapeDtypeStruct((bsz, seq, LANES), F32)),
        scratch_shapes=[pltpu.VMEM((r, d), F32), pltpu.VMEM((r, d), F32)],
        compiler_params=_cparams("parallel", "parallel"),
        name="kv_q_proj",
    )(x, mkv, mq, nkv, nq, wkv, wqg, cos, sin)


PAGE_SIZE = 128
CHUNKS_PER_PAGE = PAGE_SIZE // CMP_STRIDE
HEADS_PER_SLAB = LANES // HEAD_DIM
CMP_SLABS = 2 * KV_WIDTH // LANES


def _compress_kernel(tbl_ref, *refs, n_pages):
    page_refs = refs[:n_pages]
    pe_ref, w1_ref, b1_ref, w2_ref, out_ref = refs[n_pages:]
    m = n_pages * CHUNKS_PER_PAGE
    low = lax.broadcasted_iota(jnp.int32, (m, LANES), 1) < HEAD_DIM
    x = [jnp.concatenate([pr[0, pl.ds(l, CHUNKS_PER_PAGE, stride=CMP_STRIDE), :] for pr in page_refs], axis=0)
         for l in range(CMP_STRIDE)]
    r = [pltpu.roll(v, HEAD_DIM, 1) for v in x]
    flat = [jnp.concatenate([jnp.where(low, x[l], r[l + 1]) for l in range(0, CMP_STRIDE, 2)], axis=1),
            jnp.concatenate([jnp.where(low, r[l], x[l + 1]) for l in range(0, CMP_STRIDE, 2)], axis=1)]
    w1 = w1_ref[0]
    pb = jnp.dot(pe_ref[0], w1, preferred_element_type=F32, precision=lax.Precision.HIGHEST)
    bias = pb[0:1, :CMP_HIDDEN] + pb[1:2, CMP_HIDDEN:] + b1_ref[0]
    hid = []
    for h in range(HEADS_PER_SLAB):
        pab = _dot(flat[h].astype(BF16), w1.astype(BF16))
        nxt = pltpu.roll(pab[:, CMP_HIDDEN:], m - 1, 0)
        hid.append(_gelu(pab[:, :CMP_HIDDEN] + nxt + bias))
    out_ref[0] = _dot(jnp.concatenate(hid, axis=1).astype(BF16), w2_ref[0]).astype(out_ref.dtype)


def _compress(src, table, n_pages, pe2, w1f, b1, w2bd):
    bsz = table.shape[0] // n_pages
    m = n_pages * CHUNKS_PER_PAGE
    kern = functools.partial(_compress_kernel, n_pages=n_pages)
    slot = lambda s: s // (CMP_SLABS // 2)
    page = lambda k: pl.BlockSpec((1, PAGE_SIZE, LANES), lambda b, s, tbl: (tbl[b * n_pages + k], 0, s))
    wspec = lambda shape: pl.BlockSpec((1,) + shape, lambda b, s, tbl: (slot(s), 0, 0))
    return pl.pallas_call(
        kern,
        grid_spec=pltpu.PrefetchScalarGridSpec(
            num_scalar_prefetch=1,
            grid=(bsz, CMP_SLABS),
            in_specs=[page(k) for k in range(n_pages)] + [
                wspec(pe2.shape[1:]), wspec(w1f.shape[1:]), wspec(b1.shape[1:]), wspec(w2bd.shape[1:])],
            out_specs=pl.BlockSpec((1, m, LANES), lambda b, s, tbl: (b, 0, s)),
        ),
        out_shape=jax.ShapeDtypeStruct((bsz, m, 2 * KV_WIDTH), BF16),
        compiler_params=_cparams("parallel", "parallel"),
        name="compress_%d" % n_pages,
    )(table, *([src] * n_pages), pe2, w1f, b1, w2bd)


def _sel_matrix(nc, ns, rows, cols):
    r, q = SEL_BLOCK // CMP_STRIDE, CMP_BLOCK // CMP_STRIDE
    offs = (np.arange(r)[:, None] - np.arange(q)[None, :]).reshape(-1)
    target = r * np.arange(ns)[None, :, None] + offs[None, None, :]
    m = (np.arange(nc)[:, None, None] == target).sum(-1).astype(np.float32)
    out = np.zeros((rows, cols), np.float32)
    out[:nc, :ns] = m
    return out


def _select_mask_t(score_t, qpos, ns, n_sel):
    j = lax.broadcasted_iota(jnp.int32, score_t.shape, 0)
    cur = qpos // SEL_BLOCK
    forced = (j == 0) | (j == cur) | (j == cur - 1)
    valid = j * SEL_BLOCK <= qpos
    s = jnp.where(forced, jnp.inf, jnp.where(valid, score_t, -jnp.inf))
    s = jnp.where(j < ns, s, -jnp.inf)
    rank = jnp.zeros(score_t.shape, jnp.int32)
    for jp in range(ns):
        sp = s[jp:jp + 1, :]
        ahead = (sp > s) | ((sp == s) & (jp < j))
        rank = rank + ahead.astype(jnp.int32)
    return ((rank < n_sel) & (j < ns)).astype(F32)


def _flash_step(qh, k, v, allowed, m, l, acc):
    s = jnp.where(allowed, _dot_nt(qh, k), NEG)
    m_new = jnp.maximum(m, jnp.max(s, axis=-1, keepdims=True))
    alpha = jnp.exp(m - m_new)
    p = jnp.exp(s - m_new)
    l = alpha * l + jnp.sum(p, axis=-1, keepdims=True)
    acc = alpha * acc + _dot(p.astype(BF16), v)
    return m_new, l, acc


def _nsa_prompt_kernel(x_ref, qp_ref, qr_ref, gate_ref, cmp_ref, kvb_ref, mod_ref, npost_ref, wo_ref,
                       mselt_ref, expand_ref, y_ref, mask, o_scr, *, tq, tk, seq, ns):
    qt = pl.program_id(1)
    t0 = qt * tq
    w = KV_WIDTH
    nc_pad = cmp_ref.shape[1]
    qpos_col = t0 + lax.broadcasted_iota(jnp.int32, (tq, 1), 0)
    qpos_row = t0 + lax.broadcasted_iota(jnp.int32, (1, tq), 1)
    cends = lax.broadcasted_iota(jnp.int32, (1, nc_pad), 1) * CMP_STRIDE + (CMP_BLOCK - 1)
    cvalid = cends <= qpos_col
    n_sel_tiles = (t0 + tq + tk - 1) // tk

    for kh in range(N_KV_HEADS):
        kc = cmp_ref[0, :, kh * HEAD_DIM:(kh + 1) * HEAD_DIM]
        vc = cmp_ref[0, :, w + kh * HEAD_DIM:w + (kh + 1) * HEAD_DIM]
        psum = jnp.zeros((tq, nc_pad), F32)
        o_cmp = []
        for g in range(GQA_GROUP):
            h = kh * GQA_GROUP + g
            s = jnp.where(cvalid, _dot_nt(qp_ref[0, :, h * HEAD_DIM:(h + 1) * HEAD_DIM], kc), NEG)
            e = jnp.exp(s - jnp.max(s, axis=-1, keepdims=True))
            p = jnp.where(cvalid, e / jnp.sum(e, axis=-1, keepdims=True), 0.0)
            psum = psum + p
            o_cmp.append(_dot(p.astype(BF16), vc))
        score_t = lax.dot_general(mselt_ref[...], psum, (((1,), (1,)), ((), ())),
                                  preferred_element_type=F32, precision=lax.Precision.HIGHEST)
        sel_t = _select_mask_t(score_t, qpos_row, ns, N_SEL)
        mask[...] = lax.dot_general(sel_t.astype(BF16), expand_ref[...], (((0,), (0,)), ((), ())),
                                    preferred_element_type=F32)

        for g in range(GQA_GROUP):
            h = kh * GQA_GROUP + g
            qh = qr_ref[0, :, h * HEAD_DIM:(h + 1) * HEAD_DIM]
            init = (jnp.full((tq, 1), NEG, F32), jnp.zeros((tq, 1), F32), jnp.zeros((tq, HEAD_DIM), F32))

            def sel_step(kt, carry, qh=qh, kh=kh):
                k0 = pl.multiple_of(kt * tk, tk)
                k = kvb_ref[0, pl.ds(k0, tk), kh * HEAD_DIM:(kh + 1) * HEAD_DIM]
                v = kvb_ref[0, pl.ds(k0, tk), w + kh * HEAD_DIM:w + (kh + 1) * HEAD_DIM]
                kpos = k0 + lax.broadcasted_iota(jnp.int32, (1, tk), 1)
                allowed = (mask[:, pl.ds(k0, tk)] > 0.5) & (kpos <= qpos_col)
                return _flash_step(qh, k, v, allowed, *carry)

            m, l, acc = lax.fori_loop(0, n_sel_tiles, sel_step, init)
            o_sel = acc / l

            def win_step(i, carry, qh=qh, kh=kh):
                kstart = t0 + tq - tk - i * tk
                k0 = pl.multiple_of(jnp.maximum(kstart, 0), tk)
                k = kvb_ref[0, pl.ds(k0, tk), 2 * w + kh * HEAD_DIM:2 * w + (kh + 1) * HEAD_DIM]
                v = kvb_ref[0, pl.ds(k0, tk), 3 * w + kh * HEAD_DIM:3 * w + (kh + 1) * HEAD_DIM]
                kpos = kstart + lax.broadcasted_iota(jnp.int32, (1, tk), 1)
                diff = qpos_col - kpos
                allowed = (diff >= 0) & (diff < WINDOW) & (kpos >= 0)
                return _flash_step(qh, k, v, allowed, *carry)

            n_win_tiles = (WINDOW + tq + tk - 1) // tk
            m, l, acc = lax.fori_loop(0, n_win_tiles, win_step, init)
            o_win = acc / l
            gc = gate_ref[0, :, h:h + 1]
            gs = gate_ref[0, :, N_HEADS + h:N_HEADS + h + 1]
            gw = gate_ref[0, :, 2 * N_HEADS + h:2 * N_HEADS + h + 1]
            o_scr[:, h * HEAD_DIM:(h + 1) * HEAD_DIM] = gc * o_cmp[g] + gs * o_sel + gw * o_win

    out = _dot(o_scr[...].astype(BF16), wo_ref[...])
    y_ref[0] = x_ref[0] + mod_ref[0] * _rms(out, npost_ref[...])


def _nsa_prompt(x, qp, qr, gates, cmp, kvb, gate_mod, npost, wo, tq, tk):
    bsz, seq, d = x.shape
    tq, tk = min(tq, seq), min(tk, seq)
    assert tq % tk == 0 and seq % tq == 0
    nc = (seq - CMP_BLOCK) // CMP_STRIDE + 1
    ns = -(-seq // SEL_BLOCK)
    nc_pad = cmp.shape[1]
    j_pad = -(-ns // SUBLANES) * SUBLANES
    mselt = jnp.asarray(_sel_matrix(nc, ns, nc_pad, j_pad).T)
    expand = jnp.asarray((np.arange(j_pad)[:, None] == np.arange(seq)[None, :] // SEL_BLOCK), BF16)
    kern = functools.partial(_nsa_prompt_kernel, tq=tq, tk=tk, seq=seq, ns=ns)
    full = lambda shape: pl.BlockSpec(shape, lambda b, i: (0,) * len(shape))
    tile = lambda n: pl.BlockSpec((1, tq, n), lambda b, i: (b, i, 0))
    per_b = lambda r, n: pl.BlockSpec((1, r, n), lambda b, i: (b, 0, 0))
    return pl.pallas_call(
        kern,
        grid=(bsz, seq // tq),
        in_specs=[tile(d), tile(d), tile(d), tile(LANES), per_b(nc_pad, cmp.shape[2]), per_b(seq, kvb.shape[2]),
                  per_b(1, d), full((1, d)), full(wo.shape), full(mselt.shape), full(expand.shape)],
        out_specs=tile(d),
        out_shape=jax.ShapeDtypeStruct((bsz, seq, d), F32),
        scratch_shapes=[pltpu.VMEM((tq, seq), F32), pltpu.VMEM((tq, d), F32)],
        compiler_params=_cparams("parallel", "arbitrary"),
        name="nsa_prompt",
    )(x, qp, qr, gates, cmp, kvb, gate_mod, npost, wo, mselt, expand)


def _nsa_sample_kernel(tbl_ref, *refs, n_pages, pages_per_step, past_len, n_steps, ns):
    page_refs = refs[:pages_per_step]
    (qp_ref, qr_ref, gate_ref, cmp_ref, new_ref, wc_ref, mselt_ref, expand_ref, expand_new_ref, out_ref,
     selt, m_s, l_s, acc_s, ocmp_s) = refs[pages_per_step:]
    pc = pl.program_id(1)
    w = KV_WIDTH
    nrow = qp_ref.shape[1]
    step_of_row = lax.broadcasted_iota(jnp.int32, (nrow, 1), 0) % n_steps
    qpos_col = past_len + step_of_row
    qr = qr_ref[0]

    @pl.when(pc == 0)
    def _():
        nc_pad = cmp_ref.shape[1]
        cends = lax.broadcasted_iota(jnp.int32, (1, nc_pad), 1) * CMP_STRIDE + (CMP_BLOCK - 1)
        cvalid = cends <= qpos_col
        s = jnp.where(cvalid, _dot_nt(qp_ref[0], cmp_ref[0, :, :w]), NEG)
        e = jnp.exp(s - jnp.max(s, axis=-1, keepdims=True))
        p = jnp.where(cvalid, e / jnp.sum(e, axis=-1, keepdims=True), 0.0)
        ocmp_s[...] = _dot(p.astype(BF16), cmp_ref[0, :, w:])
        grp = nrow // GQA_GROUP
        psum = p[0:grp]
        for g in range(1, GQA_GROUP):
            psum = psum + p[g * grp:(g + 1) * grp]
        psum = jnp.concatenate([psum] * GQA_GROUP, axis=0)
        score_t = lax.dot_general(mselt_ref[...], psum, (((1,), (1,)), ((), ())),
                                  preferred_element_type=F32, precision=lax.Precision.HIGHEST)
        qpos_row = past_len + lax.broadcasted_iota(jnp.int32, (1, nrow), 1) % n_steps
        selt[...] = _select_mask_t(score_t, qpos_row, ns, N_SEL).astype(BF16)
        m_s[...] = jnp.full(m_s.shape, NEG, F32)
        l_s[...] = jnp.zeros(l_s.shape, F32)
        acc_s[...] = jnp.zeros(acc_s.shape, F32)

    def flash(k, v, allowed):
        m, l, acc = _flash_step(qr, k, v, allowed, m_s[...], l_s[...], acc_s[...])
        m_s[...], l_s[...], acc_s[...] = m, l, acc

    k = jnp.concatenate([pr[0, :, :w] for pr in page_refs], axis=0).astype(BF16)
    v = jnp.concatenate([pr[0, :, w:] for pr in page_refs], axis=0).astype(BF16)
    allowed = lax.dot_general(selt[...], expand_ref[...], (((0,), (0,)), ((), ())),
                              preferred_element_type=F32) > 0.5
    flash(k, v, allowed)

    @pl.when(pc == pl.num_programs(1) - 1)
    def _():
        pad = jnp.zeros((LANES - n_steps, 4 * w), F32)
        new = jnp.concatenate([new_ref[0].astype(F32), pad], axis=0).astype(BF16)
        key_step = lax.broadcasted_iota(jnp.int32, (1, LANES), 1)
        causal_new = key_step <= step_of_row
        picked = lax.dot_general(selt[...], expand_new_ref[...], (((0,), (0,)), ((), ())),
                                 preferred_element_type=F32) > 0.5
        flash(new[:, :w], new[:, w:2 * w], causal_new & picked)
        o_sel = acc_s[...] / l_s[...]

        w_buf = wc_ref.shape[1]
        kw = jnp.concatenate([wc_ref[0, :, :w].astype(BF16), new[:, 2 * w:3 * w]], axis=0)
        vw = jnp.concatenate([wc_ref[0, :, w:].astype(BF16), new[:, 3 * w:]], axis=0)
        buf_idx = lax.broadcasted_iota(jnp.int32, (1, w_buf), 1)
        diff_buf = step_of_row + w_buf - buf_idx
        ok = jnp.concatenate([jnp.broadcast_to(diff_buf < WINDOW, (nrow, w_buf)), causal_new], axis=1)
        sw = jnp.where(ok, _dot_nt(qr, kw), NEG)
        ew = jnp.exp(sw - jnp.max(sw, axis=-1, keepdims=True))
        o_win = _dot((ew / jnp.sum(ew, axis=-1, keepdims=True)).astype(BF16), vw)
        g = gate_ref[0]
        out_ref[0] = g[:, 0:1] * ocmp_s[...] + g[:, 1:2] * o_sel + g[:, 2:3] * o_win


SAMPLE_PAGES_PER_STEP = 8


def _nsa_sample(pool, table, n_pages, qp_bd, qr_bd, gates_r, cmp, new_kvb, wcache, past_len):
    bsz, nrow, _ = qp_bd.shape
    n_steps = new_kvb.shape[1]
    pps = min(SAMPLE_PAGES_PER_STEP, n_pages)
    nc = (past_len + n_steps - CMP_BLOCK) // CMP_STRIDE + 1
    ns = -(-(past_len + n_steps) // SEL_BLOCK)
    assert ns == past_len // SEL_BLOCK + 1 and n_steps <= SEL_BLOCK
    nc_pad = cmp.shape[1]
    j_pad = -(-ns // 16) * 16
    mselt = jnp.asarray(_sel_matrix(nc, ns, nc_pad, j_pad).T)
    expand = jnp.asarray((np.arange(j_pad)[:, None] == np.arange(past_len)[None, :] // SEL_BLOCK), BF16)
    expand_new = jnp.asarray(np.broadcast_to(np.arange(j_pad)[:, None] == ns - 1, (j_pad, LANES)), BF16)
    kern = functools.partial(_nsa_sample_kernel, n_pages=n_pages, pages_per_step=pps, past_len=past_len,
                             n_steps=n_steps, ns=ns)
    sel_cols = 1
    page = lambda k: pl.BlockSpec((1, PAGE_SIZE, 2 * KV_WIDTH),
                                  lambda b, c, tbl: (tbl[b * n_pages + c * pps + k], 0, sel_cols))
    per_b = lambda r, n: pl.BlockSpec((1, r, n), lambda b, c, tbl: (b, 0, 0))
    full = lambda shape: pl.BlockSpec(shape, lambda b, c, tbl: (0,) * len(shape))
    return pl.pallas_call(
        kern,
        grid_spec=pltpu.PrefetchScalarGridSpec(
            num_scalar_prefetch=1,
            grid=(bsz, n_pages // pps),
            in_specs=[page(k) for k in range(pps)] + [
                per_b(nrow, KV_WIDTH), per_b(nrow, KV_WIDTH), per_b(nrow, LANES), per_b(nc_pad, 2 * KV_WIDTH),
                per_b(n_steps, 4 * KV_WIDTH), per_b(wcache.shape[1], 2 * KV_WIDTH), full(mselt.shape),
                pl.BlockSpec((j_pad, pps * PAGE_SIZE), lambda b, c, tbl: (0, c)), full(expand_new.shape)],
            out_specs=per_b(nrow, KV_WIDTH),
            scratch_shapes=[pltpu.VMEM((j_pad, nrow), BF16), pltpu.VMEM((nrow, 1), F32),
                            pltpu.VMEM((nrow, 1), F32), pltpu.VMEM((nrow, KV_WIDTH), F32),
                            pltpu.VMEM((nrow, KV_WIDTH), F32)],
        ),
        out_shape=jax.ShapeDtypeStruct((bsz, nrow, KV_WIDTH), F32),
        compiler_params=_cparams("parallel", "arbitrary"),
        name="nsa_sample",
    )(table, *([pool] * pps), qp_bd, qr_bd, gates_r, cmp, new_kvb, wcache, mselt, expand, expand_new)


def _oproj_kernel(o_ref, x_ref, gate_ref, npost_ref, wo_ref, y_ref):
    out = _dot(o_ref[...].astype(BF16), wo_ref[...])
    y_ref[...] = x_ref[...] + gate_ref[...] * _rms(out, npost_ref[...])


def _oproj(o, x, gate, npost, wo):
    return pl.pallas_call(
        _oproj_kernel,
        out_shape=jax.ShapeDtypeStruct(x.shape, F32),
        compiler_params=pltpu.CompilerParams(vmem_limit_bytes=VMEM_LIMIT),
        name="nsa_out_proj",
    )(o, x, gate, npost, wo)


S5_T = 64
FFN_T = 128
FFN_TN = 256


def _prepare(p):
    w = {}
    a_re, a_im, zb_re, zb_im = _s5_prep(p['ssm_lam_re'][0], p['ssm_lam_im'][0], p['ssm_log_dt'][0],
                                        p['ssm_b_re'][0], p['ssm_b_im'][0])
    w['s5_a'] = jnp.stack([a_re.reshape(-1), a_im.reshape(-1)])
    w['s5_wb'] = jnp.concatenate([_s5_block_diag_in(zb_re), _s5_block_diag_in(zb_im)], axis=-1).astype(BF16)
    w['s5_wcre'] = _s5_block_diag_out(p['ssm_c_re'][0]).astype(BF16)
    w['s5_wcim'] = _s5_block_diag_out(p['ssm_c_im'][0]).astype(BF16)
    w['w_glu'] = p['w_glu'][0].astype(BF16)
    w['ffn_wa'] = p['ffn_w_up'][:, :, :D_FF].astype(BF16)
    w['ffn_wg'] = p['ffn_w_up'][:, :, D_FF:].astype(BF16)
    w['ffn_wd'] = p['ffn_w_down'].astype(BF16)
    w['w_kv'] = p['w_kv'].astype(BF16)
    w['w_qg'] = jnp.pad(p['w_qg'][0], ((0, 0), (0, QG_PAD - p['w_qg'].shape[2]))).astype(BF16)
    w['w_o'] = p['w_o'][0].astype(BF16)
    half = CMP_BLOCK // 2
    w1r = p['cmp_w1'].reshape(2, CMP_BLOCK, HEAD_DIM, CMP_HIDDEN)
    w['cmp_w1'] = jnp.concatenate([w1r[:, :half], w1r[:, half:]], axis=-1).reshape(
        2, half * HEAD_DIM, 2 * CMP_HIDDEN)
    w['cmp_pe'] = p['cmp_pe'].reshape(2, 2, half * HEAD_DIM)
    w['cmp_b1'] = p['cmp_b1'].reshape(2, 1, CMP_HIDDEN)
    eye = jnp.eye(HEADS_PER_SLAB, dtype=F32)
    w['cmp_w2'] = jnp.einsum('shd,ab->sahbd', p['cmp_w2'], eye).reshape(
        2, HEADS_PER_SLAB * CMP_HIDDEN, LANES).astype(BF16)
    return w


def _split3(m):
    return m.reshape(m.shape[0], 3, D_MODEL).transpose(1, 0, 2)


def _ffn_call(x, mod, layer, p, w, prefix):
    bsz = x.shape[0]
    halo = (CONV_WIDTH - 1) * bsz
    pf = prefix.transpose(1, 0, 2).reshape(halo, 2 * D_FF)
    cw, cb = p['ffn_conv_w'][layer], p['ffn_conv_b'][layer].reshape(1, -1)
    y, sta, stg = _ffn_layer(x, mod, p['norm_pre'][layer, 1:2], p['norm_post'][layer, 1:2],
                             w['ffn_wa'][layer], w['ffn_wg'][layer], cw[:, :D_FF], cw[:, D_FF:],
                             cb[:, :D_FF], cb[:, D_FF:], w['ffn_wd'][layer], pf[:, :D_FF], pf[:, D_FF:],
                             FFN_T, FFN_TN)
    st = jnp.stack([sta, stg]).reshape(2, D_FF // FFN_TN, CONV_WIDTH - 1, bsz, FFN_TN)
    st = st.transpose(3, 2, 0, 1, 4).reshape(bsz, CONV_WIDTH - 1, 2 * D_FF)
    return y, st


def _layer0(x, mods, p, w, h0, prefix):
    bsz = x.shape[0]
    y, s = _s5_layer(x, _split3(mods[0]), p['norm_pre'][0, 0:1], p['norm_post'][0, 0:1], w['s5_wb'],
                     w['s5_wcre'], w['s5_wcim'], w['s5_a'], p['ssm_d'][0:1], w['w_glu'], p['b_glu'][0:1],
                     h0, S5_T)
    s = s.reshape(2, bsz, N_GROUPS, SSM_STATE)
    y, st = _ffn_call(y, _split3(mods[1]), 0, p, w, prefix)
    return y, s[0], s[1], st


PROJ_T = 512
NSA_TQ = 256
NSA_TK = 256


def _proj_call(x, mods, modkv, p, w, pos, bt):
    bsz, seq, _ = x.shape
    cos, sin = _rope_tables(pos)
    if bt > 1:
        cos, sin = jnp.tile(cos, (bt, 1)), jnp.tile(sin, (bt, 1))
    mkv = modkv.reshape(bsz, 2, 1, D_MODEL).transpose(1, 0, 2, 3)
    mq = _split3(mods[2]).reshape(3, bsz, 1, D_MODEL)
    return _proj(x, mkv, mq, p['norm_kv'].reshape(1, D_MODEL), p['norm_pre'][1, 0:1], w['w_kv'], w['w_qg'],
                 cos, sin, bt, PROJ_T)


def _prompt_group(x, mods, modkv, p, w):
    bsz, seq, _ = x.shape
    zeros_state = jnp.zeros((2, bsz, N_STATE), F32)
    zeros_prefix = jnp.zeros((bsz, CONV_WIDTH - 1, 2 * D_FF), F32)
    x1, s_re, s_im, conv0 = _layer0(x, mods, p, w, zeros_state, zeros_prefix)
    rows, win, kvb, qp, qr, gates = _proj_call(x1, mods, modkv, p, w, np.arange(seq), 1)
    n_pages = seq // PAGE_SIZE
    table = jnp.arange(bsz * n_pages, dtype=jnp.int32)
    cmp = _compress(rows.reshape(bsz * n_pages, PAGE_SIZE, 4 * KV_WIDTH), table, n_pages,
                    w['cmp_pe'], w['cmp_w1'], w['cmp_b1'], w['cmp_w2'])
    gate_mod = _split3(mods[2])[2].reshape(bsz, 1, D_MODEL)
    x2 = _nsa_prompt(x1, qp, qr, gates, cmp, kvb, gate_mod, p['norm_post'][1, 0:1], w['w_o'], NSA_TQ, NSA_TK)
    y, conv1 = _ffn_call(x2, _split3(mods[3]), 1, p, w, zeros_prefix)
    kvd = (N_KV_HEADS, HEAD_DIM)
    return (y, rows.reshape(bsz, seq, 4, *kvd), win[:, -min(WINDOW, seq):].reshape(bsz, -1, 2, *kvd),
            s_re[None], s_im[None], jnp.stack([conv0, conv1]))


def _block_diag_rows(q):
    bsz, steps, _ = q.shape
    q5 = q.reshape(bsz, steps, N_KV_HEADS, GQA_GROUP, HEAD_DIM)
    eye = jnp.eye(N_KV_HEADS, dtype=q.dtype)
    return jnp.einsum('bqkgd,kj->bgkqjd', q5, eye).reshape(bsz, GQA_GROUP * N_KV_HEADS * steps, KV_WIDTH)


def _sample_group(x, mods, modkv, p, w, pool, page_table, wcache, ssm_re, ssm_im, conv_state):
    bsz, steps, _ = x.shape
    n_pages = page_table.shape[1]
    past_len = n_pages * PAGE_SIZE
    h0 = jnp.stack([ssm_re[0].reshape(bsz, N_STATE), ssm_im[0].reshape(bsz, N_STATE)])
    x1, s_re, s_im, conv0 = _layer0(x, mods, p, w, h0, conv_state[0])
    rows, win, kvb, qp, qr, gates = _proj_call(x1, mods, modkv, p, w, past_len + np.arange(steps), bsz)
    pool2 = pool.reshape(pool.shape[0], PAGE_SIZE, 4 * KV_WIDTH)
    table = page_table.reshape(-1)
    cmp = _compress(pool2, table, n_pages, w['cmp_pe'], w['cmp_w1'], w['cmp_b1'], w['cmp_w2'])
    nrow = GQA_GROUP * N_KV_HEADS * steps
    g5 = gates[:, :, :3 * N_HEADS].reshape(bsz, steps, 3, N_KV_HEADS, GQA_GROUP)
    gates_r = jnp.pad(g5.transpose(0, 4, 3, 1, 2).reshape(bsz, nrow, 3), ((0, 0), (0, 0), (0, LANES - 3)))
    o_bd = _nsa_sample(pool2, table, n_pages, _block_diag_rows(qp), _block_diag_rows(qr), gates_r, cmp, kvb,
                       wcache.reshape(bsz, wcache.shape[1], 2 * KV_WIDTH), past_len)
    o6 = o_bd.reshape(bsz, GQA_GROUP, N_KV_HEADS, steps, N_KV_HEADS, HEAD_DIM)
    o = jnp.einsum('bgkqjd,kj->bqkgd', o6, jnp.eye(N_KV_HEADS, dtype=F32)).reshape(bsz * steps, D_MODEL)
    gate_mod = jnp.repeat(_split3(mods[2])[2], steps, axis=0)
    x2 = _oproj(o, x1.reshape(bsz * steps, D_MODEL), gate_mod, p['norm_post'][1, 0:1], w['w_o'])
    y, conv1 = _ffn_call(x2.reshape(bsz, steps, D_MODEL), _split3(mods[3]), 1, p, w, conv_state[1])
    kvd = (N_KV_HEADS, HEAD_DIM)
    return (y, rows.reshape(bsz, steps, 4, *kvd), win.reshape(bsz, steps, 2, *kvd),
            s_re[None], s_im[None], jnp.stack([conv0, conv1]))


def kernel(x_prompt, x_sample, cache_nsa_paged, cache_nsa_window, state_ssm_re, state_ssm_im, state_ffn_conv,
           page_table, c_prompt, c_sample, norm_pre, norm_post, w_mod, b_mod, ssm_lam_re, ssm_lam_im,
           ssm_log_dt, ssm_b_re, ssm_b_im, ssm_c_re, ssm_c_im, ssm_d, w_glu, b_glu, ffn_w_up, ffn_conv_w,
           ffn_conv_b, ffn_w_down, norm_kv, w_mod_kv, b_mod_kv, w_kv, cmp_pe, cmp_w1, cmp_b1, cmp_w2,
           w_qg, w_o):
    p = {'norm_pre': norm_pre, 'norm_post': norm_post, 'w_mod': w_mod, 'b_mod': b_mod,
         'ssm_lam_re': ssm_lam_re, 'ssm_lam_im': ssm_lam_im, 'ssm_log_dt': ssm_log_dt,
         'ssm_b_re': ssm_b_re, 'ssm_b_im': ssm_b_im, 'ssm_c_re': ssm_c_re, 'ssm_c_im': ssm_c_im,
         'ssm_d': ssm_d, 'w_glu': w_glu, 'b_glu': b_glu, 'ffn_w_up': ffn_w_up, 'ffn_conv_w': ffn_conv_w,
         'ffn_conv_b': ffn_conv_b, 'ffn_w_down': ffn_w_down, 'norm_kv': norm_kv, 'w_mod_kv': w_mod_kv,
         'b_mod_kv': b_mod_kv, 'w_kv': w_kv, 'cmp_pe': cmp_pe, 'cmp_w1': cmp_w1, 'cmp_b1': cmp_b1,
         'cmp_w2': cmp_w2, 'w_qg': w_qg, 'w_o': w_o}
    w = _prepare(p)
    nb, ns = x_prompt.shape[0], x_sample.shape[0]
    c_all = jnp.concatenate([c_prompt, c_sample])
    mods = _mods(c_all, w_mod.reshape(4, D_MODEL, 3 * D_MODEL), b_mod.reshape(4, 3 * D_MODEL))
    modkv = _mods(c_all, w_mod_kv[None], b_mod_kv[None])[0]
    y_p, rows_p, win_p, sre_p, sim_p, conv_p = _prompt_group(x_prompt, mods[:, :nb], modkv[:nb], p, w)
    y_s, rows_s, win_s, sre_s, sim_s, conv_s = _sample_group(
        x_sample, mods[:, nb:], modkv[nb:], p, w, cache_nsa_paged, page_table, cache_nsa_window,
        state_ssm_re, state_ssm_im, state_ffn_conv)
    return (y_p, y_s, rows_p, rows_s, win_p, win_s, sre_p, sim_p, sre_s, sim_s, conv_p, conv_s)
```

```python
import functools
import math

import numpy as np
import jax
import jax.numpy as jnp
from jax import lax
from jax.experimental import pallas as pl
from jax.experimental.pallas import tpu as pltpu

D_MODEL = 1024
SSM_GROUP = 16
N_GROUPS = D_MODEL // SSM_GROUP
SSM_STATE = 64
N_STATE = N_GROUPS * SSM_STATE
N_HEADS = 16
HEAD_DIM = 64
N_KV_HEADS = 4
GQA_GROUP = N_HEADS // N_KV_HEADS
KV_WIDTH = N_KV_HEADS * HEAD_DIM
CMP_BLOCK = 32
CMP_STRIDE = 16
CMP_HIDDEN = 2 * HEAD_DIM
SEL_BLOCK = 64
N_SEL = 16
WINDOW = 512
ROPE_THETA = 10000.0
D_FF = 2816
CONV_WIDTH = 3
EPS = 1e-6
NEG = -1e30

SUBLANES = 8
LANES = 128
MXU_DIM = 256
VMEM_LIMIT = 56 * 1024 * 1024

F32 = jnp.float32
BF16 = jnp.bfloat16


def _cparams(*sem):
    return pltpu.CompilerParams(dimension_semantics=sem, vmem_limit_bytes=VMEM_LIMIT)


def _sigmoid(x):
    return 1.0 / (1.0 + jnp.exp(-x))


def _silu(x):
    return x * _sigmoid(x)


def _gelu(x):
    return 0.5 * x * (1.0 + jnp.tanh(math.sqrt(2.0 / math.pi) * (x + 0.044715 * (x * x * x))))


def _rms(x, g):
    return x * lax.rsqrt(jnp.mean(x * x, axis=-1, keepdims=True) + EPS) * g


def _dot(a, b):
    return jnp.dot(a, b, preferred_element_type=F32)


def _dot_nt(a, b):
    return lax.dot_general(a, b, (((1,), (1,)), ((), ())), preferred_element_type=F32)


def _mods_kernel(c_ref, w_ref, b_ref, o_ref):
    c = _silu(c_ref[...])
    o_ref[0] = jnp.dot(c, w_ref[0], preferred_element_type=F32,
                       precision=lax.Precision.HIGHEST) + b_ref[0]


def _mods(c, w, b):
    s, d, n = w.shape
    m = c.shape[0]
    tn = 1024
    return pl.pallas_call(
        _mods_kernel,
        grid=(s, n // tn),
        in_specs=[pl.BlockSpec((m, d), lambda i, j: (0, 0)),
                  pl.BlockSpec((1, d, tn), lambda i, j: (i, 0, j)),
                  pl.BlockSpec((1, 1, tn), lambda i, j: (i, 0, j))],
        out_specs=pl.BlockSpec((1, m, tn), lambda i, j: (i, 0, j)),
        out_shape=jax.ShapeDtypeStruct((s, m, n), F32),
        compiler_params=_cparams("parallel", "parallel"),
        name="adaln_mods",
    )(c, w, b.reshape(s, 1, n))


def _s5_prep_kernel(lr_ref, li_ref, ldt_ref, bre_ref, bim_ref, are_ref, aim_ref, zbre_ref, zbim_ref):
    lr, li = lr_ref[...], li_ref[...]
    dt = jnp.exp(ldt_ref[...])
    mag = jnp.exp(lr * dt)
    a_re, a_im = mag * jnp.cos(li * dt), mag * jnp.sin(li * dt)
    den = lr * lr + li * li
    ir, ii = lr / den, -li / den
    z_re = (a_re - 1.0) * ir - a_im * ii
    z_im = (a_re - 1.0) * ii + a_im * ir
    are_ref[...] = a_re
    aim_ref[...] = a_im
    bre, bim = bre_ref[...], bim_ref[...]
    zr, zi = z_re[:, None, :], z_im[:, None, :]
    zbre_ref[...] = zr * bre - zi * bim
    zbim_ref[...] = zr * bim + zi * bre


def _s5_prep(lam_re, lam_im, log_dt, b_re, b_im):
    g, p, c = b_re.shape
    return pl.pallas_call(
        _s5_prep_kernel,
        out_shape=(jax.ShapeDtypeStruct((g, p), F32), jax.ShapeDtypeStruct((g, p), F32),
                   jax.ShapeDtypeStruct((g, c, p), F32), jax.ShapeDtypeStruct((g, c, p), F32)),
        name="s5_prep",
    )(lam_re, lam_im, log_dt.reshape(g, 1), b_re.transpose(0, 2, 1), b_im.transpose(0, 2, 1))


S5_KBLOCKS = D_MODEL // MXU_DIM
S5_GPB = MXU_DIM // SSM_GROUP
S5_SPB = S5_GPB * SSM_STATE


def _s5_block_diag_in(zb):
    z = zb.reshape(S5_KBLOCKS, S5_GPB, SSM_GROUP, SSM_STATE)
    eye = jnp.eye(S5_GPB, dtype=zb.dtype)
    return jnp.einsum('kgcp,gh->kgchp', z, eye).reshape(S5_KBLOCKS, MXU_DIM, S5_SPB)


def _s5_block_diag_out(cm):
    z = cm.reshape(S5_KBLOCKS, S5_GPB, SSM_GROUP, SSM_STATE)
    eye = jnp.eye(S5_GPB, dtype=cm.dtype)
    return jnp.einsum('kgcp,gh->kgphc', z, eye).reshape(S5_KBLOCKS, S5_SPB, MXU_DIM)


S5_LANE_BLOCKS = 4


def _to_time_major(dst, b, val, t, bt):
    for c in range(val.shape[1] // LANES):
        dst[c, pl.ds(b, t, stride=bt), :] = val[:, c * LANES:(c + 1) * LANES]


def _from_time_major(src, b, t, bt):
    return jnp.concatenate([src[c, pl.ds(b, t, stride=bt), :] for c in range(src.shape[0])], axis=1)


def _slab_cols(src, c0, c1):
    return jnp.concatenate([src[c] for c in range(c0 // LANES, c1 // LANES)], axis=1)


def _s5_kernel(x_ref, mod_ref, npre_ref, npost_ref, wb_ref, wcre_ref, wcim_ref, a_ref, d_ref,
               wglu_ref, bglu_ref, h0_ref, y_ref, sout_ref, htm, bu, ytm, otm, state, *, bt, t):
    nsub = bt // SUBLANES

    @pl.when(pl.program_id(0) == 0)
    def _():
        state[...] = h0_ref[...]

    for b in range(bt):
        hb = _rms(x_ref[b], npre_ref[...]) * (1.0 + mod_ref[1, b:b + 1, :]) + mod_ref[0, b:b + 1, :]
        _to_time_major(htm, b, hb, t, bt)

    for kb in range(S5_KBLOCKS):
        cs = slice(kb * MXU_DIM, (kb + 1) * MXU_DIM)
        hkb = _slab_cols(htm, kb * MXU_DIM, (kb + 1) * MXU_DIM)
        bu[...] = _dot(hkb.astype(BF16), wb_ref[kb])
        for lb0 in range(0, S5_SPB // LANES, S5_LANE_BLOCKS):
            for sb in range(nsub):
                cols = [(lb0 + i) * LANES for i in range(S5_LANE_BLOCKS)]
                rs = slice(sb * SUBLANES, (sb + 1) * SUBLANES)
                ar = [jnp.broadcast_to(a_ref[0:1, kb * S5_SPB + c:kb * S5_SPB + c + LANES], (SUBLANES, LANES))
                      for c in cols]
                ai = [jnp.broadcast_to(a_ref[1:2, kb * S5_SPB + c:kb * S5_SPB + c + LANES], (SUBLANES, LANES))
                      for c in cols]
                init = tuple(state[0, rs, kb * S5_SPB + c:kb * S5_SPB + c + LANES] for c in cols) + \
                    tuple(state[1, rs, kb * S5_SPB + c:kb * S5_SPB + c + LANES] for c in cols)

                def step(i, carry, cols=cols, ar=ar, ai=ai, sb=sb):
                    r0 = pl.multiple_of(i * bt + sb * SUBLANES, SUBLANES)
                    out = [None] * (2 * S5_LANE_BLOCKS)
                    for n, c in enumerate(cols):
                        sr, si = carry[n], carry[S5_LANE_BLOCKS + n]
                        nr = ar[n] * sr - ai[n] * si + bu[pl.ds(r0, SUBLANES), c:c + LANES]
                        ni = ar[n] * si + ai[n] * sr + bu[pl.ds(r0, SUBLANES), S5_SPB + c:S5_SPB + c + LANES]
                        bu[pl.ds(r0, SUBLANES), c:c + LANES] = nr
                        bu[pl.ds(r0, SUBLANES), S5_SPB + c:S5_SPB + c + LANES] = ni
                        out[n], out[S5_LANE_BLOCKS + n] = nr, ni
                    return tuple(out)

                fin = lax.fori_loop(0, t, step, init, unroll=min(t, 8))
                for n, c in enumerate(cols):
                    state[0, rs, kb * S5_SPB + c:kb * S5_SPB + c + LANES] = fin[n]
                    state[1, rs, kb * S5_SPB + c:kb * S5_SPB + c + LANES] = fin[S5_LANE_BLOCKS + n]
        ykb = _dot(bu[:, :S5_SPB].astype(BF16), wcre_ref[kb]) - _dot(bu[:, S5_SPB:].astype(BF16), wcim_ref[kb])
        ytm[:, cs] = ykb + d_ref[:, cs] * hkb

    g = _gelu(ytm[...])
    out = g * _sigmoid(_dot(g.astype(BF16), wglu_ref[...]) + bglu_ref[...])
    out = _rms(out, npost_ref[...])
    for c in range(otm.shape[0]):
        otm[c] = out[:, c * LANES:(c + 1) * LANES]
    for b in range(bt):
        y_ref[b] = x_ref[b] + mod_ref[2, b:b + 1, :] * _from_time_major(otm, b, t, bt)
    sout_ref[...] = state[...]


def _s5_layer(x, mod, npre, npost, wb, wcre, wcim, a, dskip, wglu, bglu, h0, t):
    bsz, seq, d = x.shape
    t = min(t, seq)
    rows = bsz * t
    kern = functools.partial(_s5_kernel, bt=bsz, t=t)
    full = lambda shape: pl.BlockSpec(shape, lambda i: (0,) * len(shape))
    return pl.pallas_call(
        kern,
        grid=(seq // t,),
        in_specs=[pl.BlockSpec((bsz, t, d), lambda i: (0, i, 0)),
                  full((3, bsz, d)), full((1, d)), full((1, d)),
                  full(wb.shape), full(wcre.shape), full(wcim.shape), full((2, N_STATE)), full((1, d)),
                  full((d, d)), full((1, d)), full((2, bsz, N_STATE))],
        out_specs=(pl.BlockSpec((bsz, t, d), lambda i: (0, i, 0)), full((2, bsz, N_STATE))),
        out_shape=(jax.ShapeDtypeStruct((bsz, seq, d), F32), jax.ShapeDtypeStruct((2, bsz, N_STATE), F32)),
        scratch_shapes=[pltpu.VMEM((d // LANES, rows, LANES), F32), pltpu.VMEM((rows, 2 * S5_SPB), F32),
                        pltpu.VMEM((rows, d), F32), pltpu.VMEM((d // LANES, rows, LANES), F32),
                        pltpu.VMEM((2, bsz, N_STATE), F32)],
        compiler_params=_cparams("arbitrary"),
        name="s5_layer",
    )(x, mod, npre, npost, wb, wcre, wcim, a, dskip, wglu, bglu, h0)


def _ffn_kernel(x_ref, mod_ref, npre_ref, npost_ref, wa_ref, wg_ref, cwa_ref, cwg_ref, cba_ref, cbg_ref,
                wd_ref, pfa_ref, pfg_ref, y_ref, sta_ref, stg_ref, htm, hbf, ea, eg, acc, cara, carg, *, bt, t):
    i, j = pl.program_id(0), pl.program_id(1)
    rows = bt * t
    halo = (CONV_WIDTH - 1) * bt

    @pl.when(j == 0)
    def _():
        for b in range(bt):
            hb = _rms(x_ref[b], npre_ref[...]) * (1.0 + mod_ref[1, b:b + 1, :]) + mod_ref[0, b:b + 1, :]
            _to_time_major(htm, b, hb, t, bt)
        hbf[...] = _slab_cols(htm, 0, D_MODEL).astype(BF16)
        acc[...] = jnp.zeros_like(acc)

    @pl.when(i == 0)
    def _():
        cara[j] = pfa_ref[...]
        carg[j] = pfg_ref[...]

    h = hbf[...]

    def conv(e, w_ref, car, cw_ref, cb_ref):
        e[0:halo, :] = car[j]
        e[halo:halo + rows, :] = _dot(h, w_ref[...])
        out = cb_ref[...] + cw_ref[0:1, :] * e[0:rows, :]
        for k in range(1, CONV_WIDTH):
            out = out + cw_ref[k:k + 1, :] * e[k * bt:k * bt + rows, :]
        car[j] = e[rows:rows + halo, :]
        return out

    a = conv(ea, wa_ref, cara, cwa_ref, cba_ref)
    g = conv(eg, wg_ref, carg, cwg_ref, cbg_ref)
    acc[...] += _dot((_silu(g) * a).astype(BF16), wd_ref[...])
    sta_ref[j] = cara[j]
    stg_ref[j] = carg[j]

    @pl.when(j == pl.num_programs(1) - 1)
    def _():
        out = _rms(acc[...], npost_ref[...])
        for c in range(htm.shape[0]):
            htm[c] = out[:, c * LANES:(c + 1) * LANES]
        for b in range(bt):
            y_ref[b] = x_ref[b] + mod_ref[2, b:b + 1, :] * _from_time_major(htm, b, t, bt)


def _ffn_layer(x, mod, npre, npost, wa, wg, cwa, cwg, cba, cbg, wd, pfa, pfg, t, tn):
    bsz, seq, d = x.shape
    t = min(t, seq)
    rows = bsz * t
    halo = (CONV_WIDTH - 1) * bsz
    nj = D_FF // tn
    kern = functools.partial(_ffn_kernel, bt=bsz, t=t)
    full = lambda shape: pl.BlockSpec(shape, lambda i, j: (0,) * len(shape))
    col = lambda r: pl.BlockSpec((r, tn), lambda i, j: (0, j))
    return pl.pallas_call(
        kern,
        grid=(seq // t, nj),
        in_specs=[pl.BlockSpec((bsz, t, d), lambda i, j: (0, i, 0)),
                  full((3, bsz, d)), full((1, d)), full((1, d)),
                  col(d), col(d), col(CONV_WIDTH), col(CONV_WIDTH), col(1), col(1),
                  pl.BlockSpec((tn, d), lambda i, j: (j, 0)), col(halo), col(halo)],
        out_specs=(pl.BlockSpec((bsz, t, d), lambda i, j: (0, i, 0)), full((nj, halo, tn)), full((nj, halo, tn))),
        out_shape=(jax.ShapeDtypeStruct((bsz, seq, d), F32), jax.ShapeDtypeStruct((nj, halo, tn), F32),
                   jax.ShapeDtypeStruct((nj, halo, tn), F32)),
        scratch_shapes=[pltpu.VMEM((d // LANES, rows, LANES), F32), pltpu.VMEM((rows, d), BF16),
                        pltpu.VMEM((rows + halo, tn), F32),
                        pltpu.VMEM((rows + halo, tn), F32), pltpu.VMEM((rows, d), F32),
                        pltpu.VMEM((nj, halo, tn), F32), pltpu.VMEM((nj, halo, tn), F32)],
        compiler_params=_cparams("arbitrary", "arbitrary"),
        name="conv_ffn",
    )(x, mod, npre, npost, wa, wg, cwa, cwg, cba, cbg, wd, pfa, pfg)


def _rope_tables(pos):
    half = HEAD_DIM // 2
    freqs = ROPE_THETA ** (-np.arange(half, dtype=np.float64) / half)
    ang = pos.astype(np.float64)[:, None] * freqs[None, :]
    cos = np.concatenate([np.cos(ang), np.cos(ang)], axis=1)
    sin = np.concatenate([-np.sin(ang), np.sin(ang)], axis=1)
    rep = LANES // HEAD_DIM
    return (jnp.asarray(np.tile(cos, (1, rep)), F32), jnp.asarray(np.tile(sin, (1, rep)), F32))


def _rope(x, cos, sin):
    half = HEAD_DIM // 2
    first = (lax.broadcasted_iota(jnp.int32, cos.shape, 1) % HEAD_DIM) < half
    out = []
    for c in range(x.shape[1] // LANES):
        s = x[:, c * LANES:(c + 1) * LANES]
        swapped = jnp.where(first, pltpu.roll(s, LANES - half, 1), pltpu.roll(s, half, 1))
        out.append(s * cos + swapped * sin)
    return jnp.concatenate(out, axis=1)


QG_PAD = N_HEADS * HEAD_DIM + LANES


def _proj_kernel(x_ref, mkv_ref, mq_ref, nkv_ref, nq_ref, wkv_ref, wqg_ref, cos_ref, sin_ref,
                 rows_ref, win_ref, kvb_ref, qp_ref, qr_ref, gate_ref, hkv, hq, *, bt, t):
    for b in range(bt):
        xb = x_ref[b]
        xn = xb * lax.rsqrt(jnp.mean(xb * xb, axis=-1, keepdims=True) + EPS)
        hkv[b * t:(b + 1) * t, :] = (xn * nkv_ref[...]) * (1.0 + mkv_ref[1, b]) + mkv_ref[0, b]
        hq[b * t:(b + 1) * t, :] = (xn * nq_ref[...]) * (1.0 + mq_ref[1, b]) + mq_ref[0, b]
    cos, sin = cos_ref[...], sin_ref[...]
    kv = _dot(hkv[...].astype(BF16), wkv_ref[...])
    w = KV_WIDTH
    ksel = _rope(kv[:, 2 * w:3 * w], cos, sin)
    kwin = _rope(kv[:, 4 * w:5 * w], cos, sin)
    rows = jnp.concatenate([kv[:, :2 * w], ksel, kv[:, 3 * w:4 * w]], axis=1)
    win = jnp.concatenate([kwin, kv[:, 5 * w:]], axis=1)
    qg = _dot(hq[...].astype(BF16), wqg_ref[...])
    q = qg[:, :N_HEADS * HEAD_DIM] * (HEAD_DIM ** -0.5)
    qrot = _rope(q, cos, sin)
    gates = _sigmoid(qg[:, N_HEADS * HEAD_DIM:])
    for b in range(bt):
        rs = slice(b * t, (b + 1) * t)
        rows_ref[b] = rows[rs]
        win_ref[b] = win[rs]
        kvb_ref[b] = jnp.concatenate([rows[rs, 2 * w:], win[rs]], axis=1).astype(BF16)
        qp_ref[b] = q[rs].astype(BF16)
        qr_ref[b] = qrot[rs].astype(BF16)
        gate_ref[b] = gates[rs]


def _proj(x, mkv, mq, nkv, nq, wkv, wqg, cos, sin, bt, t):
    bsz, seq, d = x.shape
    t = min(t, seq)
    r = bt * t
    kern = functools.partial(_proj_kernel, bt=bt, t=t)
    full = lambda shape: pl.BlockSpec(shape, lambda b, i: (0,) * len(shape))
    tile = lambda n: pl.BlockSpec((bt, t, n), lambda b, i: (b, i, 0))
    mod = lambda n: pl.BlockSpec((n, bt, 1, d), lambda b, i: (0, b, 0, 0))
    tab = pl.BlockSpec((r, LANES), lambda b, i: (i, 0))
    nq_cols = N_HEADS * HEAD_DIM
    return pl.pallas_call(
        kern,
        grid=(bsz // bt, seq // t),
        in_specs=[tile(d), mod(2), mod(3), full((1, d)), full((1, d)), full(wkv.shape), full(wqg.shape),
                  tab, tab],
        out_specs=(tile(4 * KV_WIDTH), tile(2 * KV_WIDTH), tile(4 * KV_WIDTH), tile(nq_cols), tile(nq_cols),
                   tile(LANES)),
        out_shape=(jax.ShapeDtypeStruct((bsz, seq, 4 * KV_WIDTH), F32),
                   jax.ShapeDtypeStruct((bsz, seq, 2 * KV_WIDTH), F32),
                   jax.ShapeDtypeStruct((bsz, seq, 4 * KV_WIDTH), BF16),
                   jax.ShapeDtypeStruct((bsz, seq, nq_cols), BF16),
                   jax.ShapeDtypeStruct((bsz, seq, nq_cols), BF16),
                   jax.ShapeDtypeStruct((bsz, seq, LANES), F32)),
        scratch_shapes=[pltpu.VMEM((r, d), F32), pltpu.VMEM((r, d), F32)],
        compiler_params=_cparams("parallel", "parallel"),
        name="kv_q_proj",
    )(x, mkv, mq, nkv, nq, wkv, wqg, cos, sin)


PAGE_SIZE = 128
CHUNKS_PER_PAGE = PAGE_SIZE // CMP_STRIDE
HEADS_PER_SLAB = LANES // HEAD_DIM
CMP_SLABS = 2 * KV_WIDTH // LANES


def _compress_kernel(tbl_ref, *refs, n_pages, transposed):
    page_refs = refs[:n_pages]
    pe_ref, w1_ref, b1_ref, w2_ref, out_ref, xs = refs[n_pages:]
    m = n_pages * CHUNKS_PER_PAGE
    low = lax.broadcasted_iota(jnp.int32, (m, LANES), 1) < HEAD_DIM
    for k, pr in enumerate(page_refs):
        xs[k * PAGE_SIZE:(k + 1) * PAGE_SIZE, :] = pr[0].T if transposed else pr[0]
    x = [xs[pl.ds(l, m, stride=CMP_STRIDE), :] for l in range(CMP_STRIDE)]
    r = [pltpu.roll(v, HEAD_DIM, 1) for v in x]
    flat = [jnp.concatenate([jnp.where(low, x[l], r[l + 1]) for l in range(0, CMP_STRIDE, 2)], axis=1),
            jnp.concatenate([jnp.where(low, r[l], x[l + 1]) for l in range(0, CMP_STRIDE, 2)], axis=1)]
    w1 = w1_ref[0]
    pb = jnp.dot(pe_ref[0], w1, preferred_element_type=F32, precision=lax.Precision.HIGHEST)
    bias = pb[0:1, :CMP_HIDDEN] + pb[1:2, CMP_HIDDEN:] + b1_ref[0]
    hid = []
    for h in range(HEADS_PER_SLAB):
        pab = _dot(flat[h].astype(BF16), w1.astype(BF16))
        nxt = pltpu.roll(pab[:, CMP_HIDDEN:], m - 1, 0)
        hid.append(_gelu(pab[:, :CMP_HIDDEN] + nxt + bias))
    out_ref[0] = _dot(jnp.concatenate(hid, axis=1).astype(BF16), w2_ref[0]).astype(out_ref.dtype)


def _compress(src, table, n_pages, pe2, w1f, b1, w2bd, transposed):
    bsz = table.shape[0] // n_pages
    m = n_pages * CHUNKS_PER_PAGE
    kern = functools.partial(_compress_kernel, n_pages=n_pages, transposed=transposed)
    slot = lambda s: s // (CMP_SLABS // 2)
    if transposed:
        page = lambda k: pl.BlockSpec((1, LANES, PAGE_SIZE), lambda b, s, tbl: (tbl[b * n_pages + k], s, 0))
    else:
        page = lambda k: pl.BlockSpec((1, PAGE_SIZE, LANES), lambda b, s, tbl: (tbl[b * n_pages + k], 0, s))
    wspec = lambda shape: pl.BlockSpec((1,) + shape, lambda b, s, tbl: (slot(s), 0, 0))
    return pl.pallas_call(
        kern,
        grid_spec=pltpu.PrefetchScalarGridSpec(
            num_scalar_prefetch=1,
            grid=(bsz, CMP_SLABS),
            in_specs=[page(k) for k in range(n_pages)] + [
                wspec(pe2.shape[1:]), wspec(w1f.shape[1:]), wspec(b1.shape[1:]), wspec(w2bd.shape[1:])],
            out_specs=pl.BlockSpec((1, m, LANES), lambda b, s, tbl: (b, 0, s)),
            scratch_shapes=[pltpu.VMEM((n_pages * PAGE_SIZE, LANES), F32)],
        ),
        out_shape=jax.ShapeDtypeStruct((bsz, m, 2 * KV_WIDTH), BF16),
        compiler_params=_cparams("parallel", "parallel"),
        name="compress_%d" % n_pages,
    )(table, *([src] * n_pages), pe2, w1f, b1, w2bd)


def _sel_matrix(nc, ns, rows, cols):
    r, q = SEL_BLOCK // CMP_STRIDE, CMP_BLOCK // CMP_STRIDE
    offs = (np.arange(r)[:, None] - np.arange(q)[None, :]).reshape(-1)
    target = r * np.arange(ns)[None, :, None] + offs[None, None, :]
    m = (np.arange(nc)[:, None, None] == target).sum(-1).astype(np.float32)
    out = np.zeros((rows, cols), np.float32)
    out[:nc, :ns] = m
    return out


def _select_mask_t(score_t, qpos, ns, n_sel):
    j = lax.broadcasted_iota(jnp.int32, score_t.shape, 0)
    cur = qpos // SEL_BLOCK
    forced = (j == 0) | (j == cur) | (j == cur - 1)
    valid = j * SEL_BLOCK <= qpos
    s = jnp.where(forced, jnp.inf, jnp.where(valid, score_t, -jnp.inf))
    s = jnp.where(j < ns, s, -jnp.inf)
    rank = jnp.zeros(score_t.shape, jnp.int32)
    for jp in range(ns):
        sp = s[jp:jp + 1, :]
        ahead = (sp > s) | ((sp == s) & (jp < j))
        rank = rank + ahead.astype(jnp.int32)
    return ((rank < n_sel) & (j < ns)).astype(F32)


def _nsa_prompt_kernel(x_ref, qp_ref, qr_ref, gate_ref, cmp_ref, kvb_ref, mod_ref, npost_ref, wo_ref,
                       mselt_ref, expand_ref, y_ref, bias_sel, bias_win, q4_s, m_s, acc_s, o_scr,
                       *, tq, tk, seq, ns):
    qt = pl.program_id(1)
    t0 = qt * tq
    w = KV_WIDTH
    grp = GQA_GROUP
    nc_pad = cmp_ref.shape[1]
    qpos_col = t0 + lax.broadcasted_iota(jnp.int32, (tq, 1), 0)
    qpos_row = t0 + lax.broadcasted_iota(jnp.int32, (1, tq), 1)
    cends = lax.broadcasted_iota(jnp.int32, (1, nc_pad), 1) * CMP_STRIDE + (CMP_BLOCK - 1)
    cvalid = (cends <= qpos_col)[None]
    n_sel_tiles = (t0 + tq + tk - 1) // tk
    n_win_tiles = (WINDOW + tq + tk - 1) // tk

    for i in range(n_win_tiles):
        kpos = t0 + tq - (i + 1) * tk + lax.broadcasted_iota(jnp.int32, (1, tk), 1)
        diff = qpos_col - kpos
        bias_win[i] = jnp.where((diff >= 0) & (diff < WINDOW) & (kpos >= 0), 0.0, NEG)
    kpos_all = lax.broadcasted_iota(jnp.int32, (1, seq), 1)
    lane = lax.broadcasted_iota(jnp.int32, (tk, LANES), 1)

    def attend(kcol, vcol, n_tiles, start_of, bias_of, v_low):
        keep = ((lane < HEAD_DIM) == v_low).astype(F32).astype(BF16)
        m_s[...] = jnp.full(m_s.shape, NEG, F32)
        acc_s[...] = jnp.zeros(acc_s.shape, F32)

        def body(i, carry):
            k0 = pl.multiple_of(start_of(i), tk)
            k = kvb_ref[0, pl.ds(k0, tk), kcol:kcol + HEAD_DIM]
            v = kvb_ref[0, pl.ds(k0, tk), vcol:vcol + LANES] * keep + (1.0 - keep)
            for rc in range(grp * tq // NSA_ROWS):
                rows = slice(rc * NSA_ROWS, (rc + 1) * NSA_ROWS)
                s = _dot_nt(q4_s[rows, :], k) + bias_of(i, k0, (rc * NSA_ROWS) % tq)
                m_old = m_s[rows, :]
                m_new = jnp.maximum(m_old, jnp.max(s, axis=-1, keepdims=True))
                p = jnp.exp(s - jnp.concatenate([m_new] * (tk // LANES), axis=1))
                acc_s[rows, :] = jnp.exp(m_old - m_new) * acc_s[rows, :] + _dot(p.astype(BF16), v)
                m_s[rows, :] = m_new
            return carry

        lax.fori_loop(0, n_tiles, body, 0)
        acc = acc_s[...]
        if v_low:
            return acc[:, :HEAD_DIM] / acc[:, HEAD_DIM:HEAD_DIM + 1]
        return acc[:, HEAD_DIM:] / acc[:, 0:1]

    for kh in range(N_KV_HEADS):
        heads = [kh * grp + g for g in range(grp)]
        kc = cmp_ref[0, :, kh * HEAD_DIM:(kh + 1) * HEAD_DIM]
        vc = cmp_ref[0, :, w + kh * HEAD_DIM:w + (kh + 1) * HEAD_DIM]
        q4p = jnp.concatenate([qp_ref[0, :, h * HEAD_DIM:(h + 1) * HEAD_DIM] for h in heads], axis=0)
        s = jnp.where(cvalid, _dot_nt(q4p, kc).reshape(grp, tq, nc_pad), NEG)
        e = jnp.exp(s - jnp.max(s, axis=-1, keepdims=True))
        p = jnp.where(cvalid, e / jnp.sum(e, axis=-1, keepdims=True), 0.0)
        o_cmp = _dot(p.reshape(grp * tq, nc_pad).astype(BF16), vc)
        score_t = lax.dot_general(mselt_ref[...], jnp.sum(p, axis=0), (((1,), (1,)), ((), ())),
                                  preferred_element_type=F32, precision=lax.Precision.HIGHEST)
        sel_t = _select_mask_t(score_t, qpos_row, ns, N_SEL)
        picked = lax.dot_general(sel_t.astype(BF16), expand_ref[...], (((0,), (0,)), ((), ())),
                                 preferred_element_type=F32)
        bias_sel[...] = jnp.where((picked > 0.5) & (kpos_all <= qpos_col), 0.0, NEG)

        for g, h in enumerate(heads):
            q4_s[g * tq:(g + 1) * tq, :] = qr_ref[0, :, h * HEAD_DIM:(h + 1) * HEAD_DIM]
        v_low = kh % HEADS_PER_SLAB == 0
        vslab = (kh // HEADS_PER_SLAB) * LANES
        o_sel = attend(kh * HEAD_DIM, w + vslab, n_sel_tiles, lambda i: i * tk,
                       lambda i, k0, r0: bias_sel[r0:r0 + NSA_ROWS, pl.ds(k0, tk)], v_low)
        o_win = attend(2 * w + kh * HEAD_DIM, 3 * w + vslab, n_win_tiles,
                       lambda i: jnp.maximum(t0 + tq - (i + 1) * tk, 0),
                       lambda i, k0, r0: bias_win[i, r0:r0 + NSA_ROWS, :], v_low)
        for g, h in enumerate(heads):
            rs = slice(g * tq, (g + 1) * tq)
            gc = gate_ref[0, :, h:h + 1]
            gs = gate_ref[0, :, N_HEADS + h:N_HEADS + h + 1]
            gw = gate_ref[0, :, 2 * N_HEADS + h:2 * N_HEADS + h + 1]
            o_scr[:, h * HEAD_DIM:(h + 1) * HEAD_DIM] = gc * o_cmp[rs] + gs * o_sel[rs] + gw * o_win[rs]

    out = _dot(o_scr[...].astype(BF16), wo_ref[...])
    y_ref[0] = x_ref[0] + mod_ref[0] * _rms(out, npost_ref[...])


def _nsa_prompt(x, qp, qr, gates, cmp, kvb, gate_mod, npost, wo, tq, tk):
    bsz, seq, d = x.shape
    tq, tk = min(tq, seq), min(tk, seq)
    assert tq % tk == 0 and seq % tq == 0
    nc = (seq - CMP_BLOCK) // CMP_STRIDE + 1
    ns = -(-seq // SEL_BLOCK)
    nc_pad = cmp.shape[1]
    j_pad = -(-ns // SUBLANES) * SUBLANES
    mselt = jnp.asarray(_sel_matrix(nc, ns, nc_pad, j_pad).T)
    expand = jnp.asarray((np.arange(j_pad)[:, None] == np.arange(seq)[None, :] // SEL_BLOCK), BF16)
    kern = functools.partial(_nsa_prompt_kernel, tq=tq, tk=tk, seq=seq, ns=ns)
    full = lambda shape: pl.BlockSpec(shape, lambda b, i: (0,) * len(shape))
    tile = lambda n: pl.BlockSpec((1, tq, n), lambda b, i: (b, i, 0))
    per_b = lambda r, n: pl.BlockSpec((1, r, n), lambda b, i: (b, 0, 0))
    return pl.pallas_call(
        kern,
        grid=(bsz, seq // tq),
        in_specs=[tile(d), tile(d), tile(d), tile(LANES), per_b(nc_pad, cmp.shape[2]), per_b(seq, kvb.shape[2]),
                  per_b(1, d), full((1, d)), full(wo.shape), full(mselt.shape), full(expand.shape)],
        out_specs=tile(d),
        out_shape=jax.ShapeDtypeStruct((bsz, seq, d), F32),
        scratch_shapes=[pltpu.VMEM((tq, seq), F32), pltpu.VMEM(((WINDOW + tq + tk - 1) // tk, tq, tk), F32),
                        pltpu.VMEM((GQA_GROUP * tq, HEAD_DIM), BF16),
                        pltpu.VMEM((GQA_GROUP * tq, LANES), F32), pltpu.VMEM((GQA_GROUP * tq, LANES), F32),
                        pltpu.VMEM((tq, d), F32)],
        compiler_params=_cparams("parallel", "arbitrary"),
        name="nsa_prompt",
    )(x, qp, qr, gates, cmp, kvb, gate_mod, npost, wo, mselt, expand)


def _nsa_sample_kernel(tbl_ref, *refs, n_pages, pages_per_step, past_len, n_steps, ns):
    page_refs = refs[:pages_per_step]
    (qp_ref, qr_ref, gate_ref, cmp_ref, new_ref, wc_ref, mselt_ref, expand_ref, expand_new_ref, out_ref,
     selt, m_s, l_s, acc_s, ocmp_s) = refs[pages_per_step:]
    pc = pl.program_id(1)
    w = KV_WIDTH
    nrow = qp_ref.shape[1]
    step_of_row = lax.broadcasted_iota(jnp.int32, (nrow, 1), 0) % n_steps
    qpos_col = past_len + step_of_row
    qr = qr_ref[0]

    @pl.when(pc == 0)
    def _():
        nc_pad = cmp_ref.shape[1]
        cends = lax.broadcasted_iota(jnp.int32, (1, nc_pad), 1) * CMP_STRIDE + (CMP_BLOCK - 1)
        cvalid = cends <= qpos_col
        s = jnp.where(cvalid, _dot_nt(qp_ref[0], cmp_ref[0, :, :w]), NEG)
        e = jnp.exp(s - jnp.max(s, axis=-1, keepdims=True))
        p = jnp.where(cvalid, e / jnp.sum(e, axis=-1, keepdims=True), 0.0)
        ocmp_s[...] = _dot(p.astype(BF16), cmp_ref[0, :, w:])
        grp = nrow // GQA_GROUP
        psum = p[0:grp]
        for g in range(1, GQA_GROUP):
            psum = psum + p[g * grp:(g + 1) * grp]
        psum = jnp.concatenate([psum] * GQA_GROUP, axis=0)
        score_t = lax.dot_general(mselt_ref[...], psum, (((1,), (1,)), ((), ())),
                                  preferred_element_type=F32, precision=lax.Precision.HIGHEST)
        qpos_row = past_len + lax.broadcasted_iota(jnp.int32, (1, nrow), 1) % n_steps
        selt[...] = _select_mask_t(score_t, qpos_row, ns, N_SEL).astype(BF16)
        m_s[...] = jnp.full(m_s.shape, NEG, F32)
        l_s[...] = jnp.zeros(l_s.shape, F32)
        acc_s[...] = jnp.zeros(acc_s.shape, F32)

    def flash(s, allowed, pv):
        s = jnp.where(allowed, s, NEG)
        m_old = m_s[...]
        m_new = jnp.maximum(m_old, jnp.max(s, axis=-1, keepdims=True))
        alpha = jnp.exp(m_old - m_new)
        p = jnp.exp(s - m_new)
        l_s[...] = alpha * l_s[...] + jnp.sum(p, axis=-1, keepdims=True)
        acc_s[...] = alpha * acc_s[...] + pv(p.astype(BF16))
        m_s[...] = m_new

    kt = jnp.concatenate([pr[0, :w, :] for pr in page_refs], axis=1).astype(BF16)
    vt = jnp.concatenate([pr[0, w:, :] for pr in page_refs], axis=1).astype(BF16)
    allowed = lax.dot_general(selt[...], expand_ref[...], (((0,), (0,)), ((), ())),
                              preferred_element_type=F32) > 0.5
    flash(_dot(qr, kt), allowed, lambda p: _dot_nt(p, vt))

    @pl.when(pc == pl.num_programs(1) - 1)
    def _():
        pad = jnp.zeros((LANES - n_steps, 4 * w), F32)
        new = jnp.concatenate([new_ref[0].astype(F32), pad], axis=0).astype(BF16)
        key_step = lax.broadcasted_iota(jnp.int32, (1, LANES), 1)
        causal_new = key_step <= step_of_row
        picked = lax.dot_general(selt[...], expand_new_ref[...], (((0,), (0,)), ((), ())),
                                 preferred_element_type=F32) > 0.5
        flash(_dot_nt(qr, new[:, :w]), causal_new & picked, lambda p: _dot(p, new[:, w:2 * w]))
        o_sel = acc_s[...] / l_s[...]

        w_buf = wc_ref.shape[2]
        buf_idx = lax.broadcasted_iota(jnp.int32, (1, w_buf), 1)
        diff_buf = step_of_row + w_buf - buf_idx
        s_buf = jnp.where(diff_buf < WINDOW, _dot(qr, wc_ref[0, :w, :].astype(BF16)), NEG)
        s_new = jnp.where(causal_new, _dot_nt(qr, new[:, 2 * w:3 * w]), NEG)
        sw = jnp.concatenate([s_buf, s_new], axis=1)
        ew = jnp.exp(sw - jnp.max(sw, axis=-1, keepdims=True))
        pw = (ew / jnp.sum(ew, axis=-1, keepdims=True)).astype(BF16)
        o_win = _dot_nt(pw[:, :w_buf], wc_ref[0, w:, :].astype(BF16)) + _dot(pw[:, w_buf:], new[:, 3 * w:])
        g = gate_ref[0]
        out_ref[0] = g[:, 0:1] * ocmp_s[...] + g[:, 1:2] * o_sel + g[:, 2:3] * o_win


SAMPLE_PAGES_PER_STEP = 8


def _nsa_sample(pool, table, n_pages, qp_bd, qr_bd, gates_r, cmp, new_kvb, wcache, past_len):
    bsz, nrow, _ = qp_bd.shape
    n_steps = new_kvb.shape[1]
    pps = min(SAMPLE_PAGES_PER_STEP, n_pages)
    nc = (past_len + n_steps - CMP_BLOCK) // CMP_STRIDE + 1
    ns = -(-(past_len + n_steps) // SEL_BLOCK)
    assert ns == past_len // SEL_BLOCK + 1 and n_steps <= SEL_BLOCK
    nc_pad = cmp.shape[1]
    j_pad = -(-ns // 16) * 16
    mselt = jnp.asarray(_sel_matrix(nc, ns, nc_pad, j_pad).T)
    expand = jnp.asarray((np.arange(j_pad)[:, None] == np.arange(past_len)[None, :] // SEL_BLOCK), BF16)
    expand_new = jnp.asarray(np.broadcast_to(np.arange(j_pad)[:, None] == ns - 1, (j_pad, LANES)), BF16)
    kern = functools.partial(_nsa_sample_kernel, n_pages=n_pages, pages_per_step=pps, past_len=past_len,
                             n_steps=n_steps, ns=ns)
    sel_rows = 1
    page = lambda k: pl.BlockSpec((1, 2 * KV_WIDTH, PAGE_SIZE),
                                  lambda b, c, tbl: (tbl[b * n_pages + c * pps + k], sel_rows, 0))
    per_b = lambda r, n: pl.BlockSpec((1, r, n), lambda b, c, tbl: (b, 0, 0))
    full = lambda shape: pl.BlockSpec(shape, lambda b, c, tbl: (0,) * len(shape))
    return pl.pallas_call(
        kern,
        grid_spec=pltpu.PrefetchScalarGridSpec(
            num_scalar_prefetch=1,
            grid=(bsz, n_pages // pps),
            in_specs=[page(k) for k in range(pps)] + [
                per_b(nrow, KV_WIDTH), per_b(nrow, KV_WIDTH), per_b(nrow, LANES), per_b(nc_pad, 2 * KV_WIDTH),
                per_b(n_steps, 4 * KV_WIDTH), per_b(2 * KV_WIDTH, wcache.shape[2]), full(mselt.shape),
                pl.BlockSpec((j_pad, pps * PAGE_SIZE), lambda b, c, tbl: (0, c)), full(expand_new.shape)],
            out_specs=per_b(nrow, KV_WIDTH),
            scratch_shapes=[pltpu.VMEM((j_pad, nrow), BF16), pltpu.VMEM((nrow, 1), F32),
                            pltpu.VMEM((nrow, 1), F32), pltpu.VMEM((nrow, KV_WIDTH), F32),
                            pltpu.VMEM((nrow, KV_WIDTH), F32)],
        ),
        out_shape=jax.ShapeDtypeStruct((bsz, nrow, KV_WIDTH), F32),
        compiler_params=_cparams("parallel", "arbitrary"),
        name="nsa_sample",
    )(table, *([pool] * pps), qp_bd, qr_bd, gates_r, cmp, new_kvb, wcache, mselt, expand, expand_new)


def _oproj_kernel(o_ref, x_ref, gate_ref, npost_ref, wo_ref, y_ref):
    out = _dot(o_ref[...].astype(BF16), wo_ref[...])
    y_ref[...] = x_ref[...] + gate_ref[...] * _rms(out, npost_ref[...])


def _oproj(o, x, gate, npost, wo):
    return pl.pallas_call(
        _oproj_kernel,
        out_shape=jax.ShapeDtypeStruct(x.shape, F32),
        compiler_params=pltpu.CompilerParams(vmem_limit_bytes=VMEM_LIMIT),
        name="nsa_out_proj",
    )(o, x, gate, npost, wo)


S5_T = 64
FFN_T = 128
FFN_TN = 256


def _prepare(p):
    w = {}
    a_re, a_im, zb_re, zb_im = _s5_prep(p['ssm_lam_re'][0], p['ssm_lam_im'][0], p['ssm_log_dt'][0],
                                        p['ssm_b_re'][0], p['ssm_b_im'][0])
    w['s5_a'] = jnp.stack([a_re.reshape(-1), a_im.reshape(-1)])
    w['s5_wb'] = jnp.concatenate([_s5_block_diag_in(zb_re), _s5_block_diag_in(zb_im)], axis=-1).astype(BF16)
    w['s5_wcre'] = _s5_block_diag_out(p['ssm_c_re'][0]).astype(BF16)
    w['s5_wcim'] = _s5_block_diag_out(p['ssm_c_im'][0]).astype(BF16)
    w['w_glu'] = p['w_glu'][0].astype(BF16)
    w['ffn_wa'] = p['ffn_w_up'][:, :, :D_FF].astype(BF16)
    w['ffn_wg'] = p['ffn_w_up'][:, :, D_FF:].astype(BF16)
    w['ffn_wd'] = p['ffn_w_down'].astype(BF16)
    w['w_kv'] = p['w_kv'].astype(BF16)
    w['w_qg'] = jnp.pad(p['w_qg'][0], ((0, 0), (0, QG_PAD - p['w_qg'].shape[2]))).astype(BF16)
    w['w_o'] = p['w_o'][0].astype(BF16)
    half = CMP_BLOCK // 2
    w1r = p['cmp_w1'].reshape(2, CMP_BLOCK, HEAD_DIM, CMP_HIDDEN)
    w['cmp_w1'] = jnp.concatenate([w1r[:, :half], w1r[:, half:]], axis=-1).reshape(
        2, half * HEAD_DIM, 2 * CMP_HIDDEN)
    w['cmp_pe'] = p['cmp_pe'].reshape(2, 2, half * HEAD_DIM)
    w['cmp_b1'] = p['cmp_b1'].reshape(2, 1, CMP_HIDDEN)
    eye = jnp.eye(HEADS_PER_SLAB, dtype=F32)
    w['cmp_w2'] = jnp.einsum('shd,ab->sahbd', p['cmp_w2'], eye).reshape(
        2, HEADS_PER_SLAB * CMP_HIDDEN, LANES).astype(BF16)
    return w


def _split3(m):
    return m.reshape(m.shape[0], 3, D_MODEL).transpose(1, 0, 2)


def _ffn_call(x, mod, layer, p, w, prefix):
    bsz = x.shape[0]
    halo = (CONV_WIDTH - 1) * bsz
    pf = prefix.transpose(1, 0, 2).reshape(halo, 2 * D_FF)
    cw, cb = p['ffn_conv_w'][layer], p['ffn_conv_b'][layer].reshape(1, -1)
    y, sta, stg = _ffn_layer(x, mod, p['norm_pre'][layer, 1:2], p['norm_post'][layer, 1:2],
                             w['ffn_wa'][layer], w['ffn_wg'][layer], cw[:, :D_FF], cw[:, D_FF:],
                             cb[:, :D_FF], cb[:, D_FF:], w['ffn_wd'][layer], pf[:, :D_FF], pf[:, D_FF:],
                             FFN_T, FFN_TN)
    st = jnp.stack([sta, stg]).reshape(2, D_FF // FFN_TN, CONV_WIDTH - 1, bsz, FFN_TN)
    st = st.transpose(3, 2, 0, 1, 4).reshape(bsz, CONV_WIDTH - 1, 2 * D_FF)
    return y, st


def _layer0(x, mods, p, w, h0, prefix):
    bsz = x.shape[0]
    y, s = _s5_layer(x, _split3(mods[0]), p['norm_pre'][0, 0:1], p['norm_post'][0, 0:1], w['s5_wb'],
                     w['s5_wcre'], w['s5_wcim'], w['s5_a'], p['ssm_d'][0:1], w['w_glu'], p['b_glu'][0:1],
                     h0, S5_T)
    s = s.reshape(2, bsz, N_GROUPS, SSM_STATE)
    y, st = _ffn_call(y, _split3(mods[1]), 0, p, w, prefix)
    return y, s[0], s[1], st


PROJ_T = 512
NSA_TQ = 256
NSA_TK = 256
NSA_ROWS = 128


def _proj_call(x, mods, modkv, p, w, pos, bt):
    bsz, seq, _ = x.shape
    cos, sin = _rope_tables(pos)
    if bt > 1:
        cos, sin = jnp.tile(cos, (bt, 1)), jnp.tile(sin, (bt, 1))
    mkv = modkv.reshape(bsz, 2, 1, D_MODEL).transpose(1, 0, 2, 3)
    mq = _split3(mods[2]).reshape(3, bsz, 1, D_MODEL)
    return _proj(x, mkv, mq, p['norm_kv'].reshape(1, D_MODEL), p['norm_pre'][1, 0:1], w['w_kv'], w['w_qg'],
                 cos, sin, bt, PROJ_T)


def _prompt_group(x, mods, modkv, p, w):
    bsz, seq, _ = x.shape
    zeros_state = jnp.zeros((2, bsz, N_STATE), F32)
    zeros_prefix = jnp.zeros((bsz, CONV_WIDTH - 1, 2 * D_FF), F32)
    x1, s_re, s_im, conv0 = _layer0(x, mods, p, w, zeros_state, zeros_prefix)
    rows, win, kvb, qp, qr, gates = _proj_call(x1, mods, modkv, p, w, np.arange(seq), 1)
    n_pages = seq // PAGE_SIZE
    table = jnp.arange(bsz * n_pages, dtype=jnp.int32)
    cmp = _compress(rows.reshape(bsz * n_pages, PAGE_SIZE, 4 * KV_WIDTH), table, n_pages,
                    w['cmp_pe'], w['cmp_w1'], w['cmp_b1'], w['cmp_w2'], False)
    gate_mod = _split3(mods[2])[2].reshape(bsz, 1, D_MODEL)
    x2 = _nsa_prompt(x1, qp, qr, gates, cmp, kvb, gate_mod, p['norm_post'][1, 0:1], w['w_o'], NSA_TQ, NSA_TK)
    y, conv1 = _ffn_call(x2, _split3(mods[3]), 1, p, w, zeros_prefix)
    kvd = (N_KV_HEADS, HEAD_DIM)
    return (y, rows.reshape(bsz, seq, 4, *kvd), win[:, -min(WINDOW, seq):].reshape(bsz, -1, 2, *kvd),
            s_re[None], s_im[None], jnp.stack([conv0, conv1]))


def _block_diag_rows(q):
    bsz, steps, _ = q.shape
    q5 = q.reshape(bsz, steps, N_KV_HEADS, GQA_GROUP, HEAD_DIM)
    eye = jnp.eye(N_KV_HEADS, dtype=q.dtype)
    return jnp.einsum('bqkgd,kj->bgkqjd', q5, eye).reshape(bsz, GQA_GROUP * N_KV_HEADS * steps, KV_WIDTH)


def _sample_group(x, mods, modkv, p, w, pool, page_table, wcache, ssm_re, ssm_im, conv_state):
    bsz, steps, _ = x.shape
    n_pages = page_table.shape[1]
    past_len = n_pages * PAGE_SIZE
    h0 = jnp.stack([ssm_re[0].reshape(bsz, N_STATE), ssm_im[0].reshape(bsz, N_STATE)])
    x1, s_re, s_im, conv0 = _layer0(x, mods, p, w, h0, conv_state[0])
    rows, win, kvb, qp, qr, gates = _proj_call(x1, mods, modkv, p, w, past_len + np.arange(steps), bsz)
    pool2 = pool.transpose(0, 2, 3, 4, 1).reshape(pool.shape[0], 4 * KV_WIDTH, PAGE_SIZE)
    wcache2 = wcache.transpose(0, 2, 3, 4, 1).reshape(bsz, 2 * KV_WIDTH, wcache.shape[1])
    table = page_table.reshape(-1)
    cmp = _compress(pool2, table, n_pages, w['cmp_pe'], w['cmp_w1'], w['cmp_b1'], w['cmp_w2'], True)
    nrow = GQA_GROUP * N_KV_HEADS * steps
    g5 = gates[:, :, :3 * N_HEADS].reshape(bsz, steps, 3, N_KV_HEADS, GQA_GROUP)
    gates_r = jnp.pad(g5.transpose(0, 4, 3, 1, 2).reshape(bsz, nrow, 3), ((0, 0), (0, 0), (0, LANES - 3)))
    o_bd = _nsa_sample(pool2, table, n_pages, _block_diag_rows(qp), _block_diag_rows(qr), gates_r, cmp, kvb,
                       wcache2, past_len)
    o6 = o_bd.reshape(bsz, GQA_GROUP, N_KV_HEADS, steps, N_KV_HEADS, HEAD_DIM)
    o = jnp.einsum('bgkqjd,kj->bqkgd', o6, jnp.eye(N_KV_HEADS, dtype=F32)).reshape(bsz * steps, D_MODEL)
    gate_mod = jnp.repeat(_split3(mods[2])[2], steps, axis=0)
    x2 = _oproj(o, x1.reshape(bsz * steps, D_MODEL), gate_mod, p['norm_post'][1, 0:1], w['w_o'])
    y, conv1 = _ffn_call(x2.reshape(bsz, steps, D_MODEL), _split3(mods[3]), 1, p, w, conv_state[1])
    kvd = (N_KV_HEADS, HEAD_DIM)
    return (y, rows.reshape(bsz, steps, 4, *kvd), win.reshape(bsz, steps, 2, *kvd),
            s_re[None], s_im[None], jnp.stack([conv0, conv1]))


def kernel(x_prompt, x_sample, cache_nsa_paged, cache_nsa_window, state_ssm_re, state_ssm_im, state_ffn_conv,
           page_table, c_prompt, c_sample, norm_pre, norm_post, w_mod, b_mod, ssm_lam_re, ssm_lam_im,
           ssm_log_dt, ssm_b_re, ssm_b_im, ssm_c_re, ssm_c_im, ssm_d, w_glu, b_glu, ffn_w_up, ffn_conv_w,
           ffn_conv_b, ffn_w_down, norm_kv, w_mod_kv, b_mod_kv, w_kv, cmp_pe, cmp_w1, cmp_b1, cmp_w2,
           w_qg, w_o):
    p = {'norm_pre': norm_pre, 'norm_post': norm_post, 'w_mod': w_mod, 'b_mod': b_mod,
         'ssm_lam_re': ssm_lam_re, 'ssm_lam_im': ssm_lam_im, 'ssm_log_dt': ssm_log_dt,
         'ssm_b_re': ssm_b_re, 'ssm_b_im': ssm_b_im, 'ssm_c_re': ssm_c_re, 'ssm_c_im': ssm_c_im,
         'ssm_d': ssm_d, 'w_glu': w_glu, 'b_glu': b_glu, 'ffn_w_up': ffn_w_up, 'ffn_conv_w': ffn_conv_w,
         'ffn_conv_b': ffn_conv_b, 'ffn_w_down': ffn_w_down, 'norm_kv': norm_kv, 'w_mod_kv': w_mod_kv,
         'b_mod_kv': b_mod_kv, 'w_kv': w_kv, 'cmp_pe': cmp_pe, 'cmp_w1': cmp_w1, 'cmp_b1': cmp_b1,
         'cmp_w2': cmp_w2, 'w_qg': w_qg, 'w_o': w_o}
    w = _prepare(p)
    nb, ns = x_prompt.shape[0], x_sample.shape[0]
    c_all = jnp.concatenate([c_prompt, c_sample])
    mods = _mods(c_all, w_mod.reshape(4, D_MODEL, 3 * D_MODEL), b_mod.reshape(4, 3 * D_MODEL))
    modkv = _mods(c_all, w_mod_kv[None], b_mod_kv[None])[0]
    y_p, rows_p, win_p, sre_p, sim_p, conv_p = _prompt_group(x_prompt, mods[:, :nb], modkv[:nb], p, w)
    y_s, rows_s, win_s, sre_s, sim_s, conv_s = _sample_group(
        x_sample, mods[:, nb:], modkv[nb:], p, w, cache_nsa_paged, page_table, cache_nsa_window,
        state_ssm_re, state_ssm_im, state_ffn_conv)
    return (y_p, y_s, rows_p, rows_s, win_p, win_s, sre_p, sim_p, sre_s, sim_s, conv_p, conv_s)
```

```python
import functools
import math

import numpy as np
import jax
import jax.numpy as jnp
from jax import lax
from jax.experimental import pallas as pl
from jax.experimental.pallas import tpu as pltpu

D_MODEL = 1024
SSM_GROUP = 16
N_GROUPS = D_MODEL // SSM_GROUP
SSM_STATE = 64
N_STATE = N_GROUPS * SSM_STATE
N_HEADS = 16
HEAD_DIM = 64
N_KV_HEADS = 4
GQA_GROUP = N_HEADS // N_KV_HEADS
KV_WIDTH = N_KV_HEADS * HEAD_DIM
CMP_BLOCK = 32
CMP_STRIDE = 16
CMP_HIDDEN = 2 * HEAD_DIM
SEL_BLOCK = 64
N_SEL = 16
WINDOW = 512
ROPE_THETA = 10000.0
D_FF = 2816
CONV_WIDTH = 3
EPS = 1e-6
NEG = -1e30

SUBLANES = 8
LANES = 128
MXU_DIM = 256
VMEM_LIMIT = 56 * 1024 * 1024

F32 = jnp.float32
BF16 = jnp.bfloat16


def _cparams(*sem):
    return pltpu.CompilerParams(dimension_semantics=sem, vmem_limit_bytes=VMEM_LIMIT)


def _sigmoid(x):
    return 1.0 / (1.0 + jnp.exp(-x))


def _silu(x):
    return x * _sigmoid(x)


def _gelu(x):
    return 0.5 * x * (1.0 + jnp.tanh(math.sqrt(2.0 / math.pi) * (x + 0.044715 * (x * x * x))))


def _rms(x, g):
    return x * lax.rsqrt(jnp.mean(x * x, axis=-1, keepdims=True) + EPS) * g


def _dot(a, b):
    return jnp.dot(a, b, preferred_element_type=F32)


def _dot_nt(a, b):
    return lax.dot_general(a, b, (((1,), (1,)), ((), ())), preferred_element_type=F32)


def _mods_kernel(c_ref, w_ref, b_ref, o_ref):
    c = _silu(c_ref[...])
    o_ref[0] = jnp.dot(c, w_ref[0], preferred_element_type=F32,
                       precision=lax.Precision.HIGHEST) + b_ref[0]


def _mods(c, w, b):
    s, d, n = w.shape
    m = c.shape[0]
    tn = 1024
    return pl.pallas_call(
        _mods_kernel,
        grid=(s, n // tn),
        in_specs=[pl.BlockSpec((m, d), lambda i, j: (0, 0)),
                  pl.BlockSpec((1, d, tn), lambda i, j: (i, 0, j)),
                  pl.BlockSpec((1, 1, tn), lambda i, j: (i, 0, j))],
        out_specs=pl.BlockSpec((1, m, tn), lambda i, j: (i, 0, j)),
        out_shape=jax.ShapeDtypeStruct((s, m, n), F32),
        compiler_params=_cparams("parallel", "parallel"),
        name="adaln_mods",
    )(c, w, b.reshape(s, 1, n))


def _s5_prep_kernel(lr_ref, li_ref, ldt_ref, bre_ref, bim_ref, are_ref, aim_ref, zbre_ref, zbim_ref):
    lr, li = lr_ref[...], li_ref[...]
    dt = jnp.exp(ldt_ref[...])
    mag = jnp.exp(lr * dt)
    a_re, a_im = mag * jnp.cos(li * dt), mag * jnp.sin(li * dt)
    den = lr * lr + li * li
    ir, ii = lr / den, -li / den
    z_re = (a_re - 1.0) * ir - a_im * ii
    z_im = (a_re - 1.0) * ii + a_im * ir
    are_ref[...] = a_re
    aim_ref[...] = a_im
    bre, bim = bre_ref[...], bim_ref[...]
    zr, zi = z_re[:, None, :], z_im[:, None, :]
    zbre_ref[...] = zr * bre - zi * bim
    zbim_ref[...] = zr * bim + zi * bre


def _s5_prep(lam_re, lam_im, log_dt, b_re, b_im):
    g, p, c = b_re.shape
    return pl.pallas_call(
        _s5_prep_kernel,
        out_shape=(jax.ShapeDtypeStruct((g, p), F32), jax.ShapeDtypeStruct((g, p), F32),
                   jax.ShapeDtypeStruct((g, c, p), F32), jax.ShapeDtypeStruct((g, c, p), F32)),
        name="s5_prep",
    )(lam_re, lam_im, log_dt.reshape(g, 1), b_re.transpose(0, 2, 1), b_im.transpose(0, 2, 1))


S5_KBLOCKS = D_MODEL // MXU_DIM
S5_GPB = MXU_DIM // SSM_GROUP
S5_SPB = S5_GPB * SSM_STATE


def _s5_block_diag_in(zb):
    z = zb.reshape(S5_KBLOCKS, S5_GPB, SSM_GROUP, SSM_STATE)
    eye = jnp.eye(S5_GPB, dtype=zb.dtype)
    return jnp.einsum('kgcp,gh->kgchp', z, eye).reshape(S5_KBLOCKS, MXU_DIM, S5_SPB)


def _s5_block_diag_out(cm):
    z = cm.reshape(S5_KBLOCKS, S5_GPB, SSM_GROUP, SSM_STATE)
    eye = jnp.eye(S5_GPB, dtype=cm.dtype)
    return jnp.einsum('kgcp,gh->kgphc', z, eye).reshape(S5_KBLOCKS, S5_SPB, MXU_DIM)


S5_LANE_BLOCKS = 4


def _to_time_major(dst, b, val, t, bt):
    for c in range(val.shape[1] // LANES):
        dst[c, pl.ds(b, t, stride=bt), :] = val[:, c * LANES:(c + 1) * LANES]


def _from_time_major(src, b, t, bt):
    return jnp.concatenate([src[c, pl.ds(b, t, stride=bt), :] for c in range(src.shape[0])], axis=1)


def _slab_cols(src, c0, c1):
    return jnp.concatenate([src[c] for c in range(c0 // LANES, c1 // LANES)], axis=1)


def _s5_kernel(x_ref, mod_ref, npre_ref, npost_ref, wb_ref, wcre_ref, wcim_ref, a_ref, d_ref,
               wglu_ref, bglu_ref, h0_ref, y_ref, sout_ref, htm, bu, ytm, otm, state, *, bt, t):
    nsub = bt // SUBLANES

    @pl.when(pl.program_id(0) == 0)
    def _():
        state[...] = h0_ref[...]

    for b in range(bt):
        hb = _rms(x_ref[b], npre_ref[...]) * (1.0 + mod_ref[1, b:b + 1, :]) + mod_ref[0, b:b + 1, :]
        _to_time_major(htm, b, hb, t, bt)

    for kb in range(S5_KBLOCKS):
        cs = slice(kb * MXU_DIM, (kb + 1) * MXU_DIM)
        hkb = _slab_cols(htm, kb * MXU_DIM, (kb + 1) * MXU_DIM)
        bu[...] = _dot(hkb.astype(BF16), wb_ref[kb])
        for lb0 in range(0, S5_SPB // LANES, S5_LANE_BLOCKS):
            for sb in range(nsub):
                cols = [(lb0 + i) * LANES for i in range(S5_LANE_BLOCKS)]
                rs = slice(sb * SUBLANES, (sb + 1) * SUBLANES)
                ar = [jnp.broadcast_to(a_ref[0:1, kb * S5_SPB + c:kb * S5_SPB + c + LANES], (SUBLANES, LANES))
                      for c in cols]
                ai = [jnp.broadcast_to(a_ref[1:2, kb * S5_SPB + c:kb * S5_SPB + c + LANES], (SUBLANES, LANES))
                      for c in cols]
                init = tuple(state[0, rs, kb * S5_SPB + c:kb * S5_SPB + c + LANES] for c in cols) + \
                    tuple(state[1, rs, kb * S5_SPB + c:kb * S5_SPB + c + LANES] for c in cols)

                def step(i, carry, cols=cols, ar=ar, ai=ai, sb=sb):
                    r0 = pl.multiple_of(i * bt + sb * SUBLANES, SUBLANES)
                    out = [None] * (2 * S5_LANE_BLOCKS)
                    for n, c in enumerate(cols):
                        sr, si = carry[n], carry[S5_LANE_BLOCKS + n]
                        nr = ar[n] * sr - ai[n] * si + bu[pl.ds(r0, SUBLANES), c:c + LANES]
                        ni = ar[n] * si + ai[n] * sr + bu[pl.ds(r0, SUBLANES), S5_SPB + c:S5_SPB + c + LANES]
                        bu[pl.ds(r0, SUBLANES), c:c + LANES] = nr
                        bu[pl.ds(r0, SUBLANES), S5_SPB + c:S5_SPB + c + LANES] = ni
                        out[n], out[S5_LANE_BLOCKS + n] = nr, ni
                    return tuple(out)

                fin = lax.fori_loop(0, t, step, init, unroll=min(t, 8))
                for n, c in enumerate(cols):
                    state[0, rs, kb * S5_SPB + c:kb * S5_SPB + c + LANES] = fin[n]
                    state[1, rs, kb * S5_SPB + c:kb * S5_SPB + c + LANES] = fin[S5_LANE_BLOCKS + n]
        ykb = _dot(bu[:, :S5_SPB].astype(BF16), wcre_ref[kb]) - _dot(bu[:, S5_SPB:].astype(BF16), wcim_ref[kb])
        ytm[:, cs] = ykb + d_ref[:, cs] * hkb

    g = _gelu(ytm[...])
    out = g * _sigmoid(_dot(g.astype(BF16), wglu_ref[...]) + bglu_ref[...])
    out = _rms(out, npost_ref[...])
    for c in range(otm.shape[0]):
        otm[c] = out[:, c * LANES:(c + 1) * LANES]
    for b in range(bt):
        y_ref[b] = x_ref[b] + mod_ref[2, b:b + 1, :] * _from_time_major(otm, b, t, bt)
    sout_ref[...] = state[...]


def _s5_layer(x, mod, npre, npost, wb, wcre, wcim, a, dskip, wglu, bglu, h0, t):
    bsz, seq, d = x.shape
    t = min(t, seq)
    rows = bsz * t
    kern = functools.partial(_s5_kernel, bt=bsz, t=t)
    full = lambda shape: pl.BlockSpec(shape, lambda i: (0,) * len(shape))
    return pl.pallas_call(
        kern,
        grid=(seq // t,),
        in_specs=[pl.BlockSpec((bsz, t, d), lambda i: (0, i, 0)),
                  full((3, bsz, d)), full((1, d)), full((1, d)),
                  full(wb.shape), full(wcre.shape), full(wcim.shape), full((2, N_STATE)), full((1, d)),
                  full((d, d)), full((1, d)), full((2, bsz, N_STATE))],
        out_specs=(pl.BlockSpec((bsz, t, d), lambda i: (0, i, 0)), full((2, bsz, N_STATE))),
        out_shape=(jax.ShapeDtypeStruct((bsz, seq, d), F32), jax.ShapeDtypeStruct((2, bsz, N_STATE), F32)),
        scratch_shapes=[pltpu.VMEM((d // LANES, rows, LANES), F32), pltpu.VMEM((rows, 2 * S5_SPB), F32),
                        pltpu.VMEM((rows, d), F32), pltpu.VMEM((d // LANES, rows, LANES), F32),
                        pltpu.VMEM((2, bsz, N_STATE), F32)],
        compiler_params=_cparams("arbitrary"),
        name="s5_layer",
    )(x, mod, npre, npost, wb, wcre, wcim, a, dskip, wglu, bglu, h0)


def _ffn_kernel(x_ref, mod_ref, npre_ref, npost_ref, wa_ref, wg_ref, cwa_ref, cwg_ref, cba_ref, cbg_ref,
                wd_ref, pfa_ref, pfg_ref, y_ref, sta_ref, stg_ref, htm, hbf, ea, eg, acc, cara, carg, *, bt, t):
    i, j = pl.program_id(0), pl.program_id(1)
    rows = bt * t
    halo = (CONV_WIDTH - 1) * bt

    @pl.when(j == 0)
    def _():
        for b in range(bt):
            hb = _rms(x_ref[b], npre_ref[...]) * (1.0 + mod_ref[1, b:b + 1, :]) + mod_ref[0, b:b + 1, :]
            _to_time_major(htm, b, hb, t, bt)
        hbf[...] = _slab_cols(htm, 0, D_MODEL).astype(BF16)
        acc[...] = jnp.zeros_like(acc)

    @pl.when(i == 0)
    def _():
        cara[j] = pfa_ref[...]
        carg[j] = pfg_ref[...]

    h = hbf[...]

    def conv(e, w_ref, car, cw_ref, cb_ref):
        e[0:halo, :] = car[j]
        e[halo:halo + rows, :] = _dot(h, w_ref[...])
        out = cb_ref[...] + cw_ref[0:1, :] * e[0:rows, :]
        for k in range(1, CONV_WIDTH):
            out = out + cw_ref[k:k + 1, :] * e[k * bt:k * bt + rows, :]
        car[j] = e[rows:rows + halo, :]
        return out

    a = conv(ea, wa_ref, cara, cwa_ref, cba_ref)
    g = conv(eg, wg_ref, carg, cwg_ref, cbg_ref)
    acc[...] += _dot((_silu(g) * a).astype(BF16), wd_ref[...])
    sta_ref[j] = cara[j]
    stg_ref[j] = carg[j]

    @pl.when(j == pl.num_programs(1) - 1)
    def _():
        out = _rms(acc[...], npost_ref[...])
        for c in range(htm.shape[0]):
            htm[c] = out[:, c * LANES:(c + 1) * LANES]
        for b in range(bt):
            y_ref[b] = x_ref[b] + mod_ref[2, b:b + 1, :] * _from_time_major(htm, b, t, bt)


def _ffn_layer(x, mod, npre, npost, wa, wg, cwa, cwg, cba, cbg, wd, pfa, pfg, t, tn):
    bsz, seq, d = x.shape
    t = min(t, seq)
    rows = bsz * t
    halo = (CONV_WIDTH - 1) * bsz
    nj = D_FF // tn
    kern = functools.partial(_ffn_kernel, bt=bsz, t=t)
    full = lambda shape: pl.BlockSpec(shape, lambda i, j: (0,) * len(shape))
    col = lambda r: pl.BlockSpec((r, tn), lambda i, j: (0, j))
    return pl.pallas_call(
        kern,
        grid=(seq // t, nj),
        in_specs=[pl.BlockSpec((bsz, t, d), lambda i, j: (0, i, 0)),
                  full((3, bsz, d)), full((1, d)), full((1, d)),
                  col(d), col(d), col(CONV_WIDTH), col(CONV_WIDTH), col(1), col(1),
                  pl.BlockSpec((tn, d), lambda i, j: (j, 0)), col(halo), col(halo)],
        out_specs=(pl.BlockSpec((bsz, t, d), lambda i, j: (0, i, 0)), full((nj, halo, tn)), full((nj, halo, tn))),
        out_shape=(jax.ShapeDtypeStruct((bsz, seq, d), F32), jax.ShapeDtypeStruct((nj, halo, tn), F32),
                   jax.ShapeDtypeStruct((nj, halo, tn), F32)),
        scratch_shapes=[pltpu.VMEM((d // LANES, rows, LANES), F32), pltpu.VMEM((rows, d), BF16),
                        pltpu.VMEM((rows + halo, tn), F32),
                        pltpu.VMEM((rows + halo, tn), F32), pltpu.VMEM((rows, d), F32),
                        pltpu.VMEM((nj, halo, tn), F32), pltpu.VMEM((nj, halo, tn), F32)],
        compiler_params=_cparams("arbitrary", "arbitrary"),
        name="conv_ffn",
    )(x, mod, npre, npost, wa, wg, cwa, cwg, cba, cbg, wd, pfa, pfg)


def _rope_tables(pos):
    half = HEAD_DIM // 2
    freqs = ROPE_THETA ** (-np.arange(half, dtype=np.float64) / half)
    ang = pos.astype(np.float64)[:, None] * freqs[None, :]
    cos = np.concatenate([np.cos(ang), np.cos(ang)], axis=1)
    sin = np.concatenate([-np.sin(ang), np.sin(ang)], axis=1)
    rep = LANES // HEAD_DIM
    return (jnp.asarray(np.tile(cos, (1, rep)), F32), jnp.asarray(np.tile(sin, (1, rep)), F32))


def _rope(x, cos, sin):
    half = HEAD_DIM // 2
    first = (lax.broadcasted_iota(jnp.int32, cos.shape, 1) % HEAD_DIM) < half
    out = []
    for c in range(x.shape[1] // LANES):
        s = x[:, c * LANES:(c + 1) * LANES]
        swapped = jnp.where(first, pltpu.roll(s, LANES - half, 1), pltpu.roll(s, half, 1))
        out.append(s * cos + swapped * sin)
    return jnp.concatenate(out, axis=1)


QG_PAD = N_HEADS * HEAD_DIM + LANES
NSA_TQ = 256
NSA_TK = 128
NSA_COLS = 256


def _proj_kernel(x_ref, mkv_ref, mq_ref, nkv_ref, nq_ref, wkv_ref, wqg_ref, cos_ref, sin_ref,
                 rows_ref, win_ref, kvb_ref, qp_ref, qr_ref, gate_ref, *rest, bt, t, transposed):
    if transposed:
        vt_ref, hkv, hq = rest
    else:
        hkv, hq = rest
    for b in range(bt):
        xb = x_ref[b]
        xn = xb * lax.rsqrt(jnp.mean(xb * xb, axis=-1, keepdims=True) + EPS)
        hkv[b * t:(b + 1) * t, :] = (xn * nkv_ref[...]) * (1.0 + mkv_ref[1, b]) + mkv_ref[0, b]
        hq[b * t:(b + 1) * t, :] = (xn * nq_ref[...]) * (1.0 + mq_ref[1, b]) + mq_ref[0, b]
    cos, sin = cos_ref[...], sin_ref[...]
    kv = _dot(hkv[...].astype(BF16), wkv_ref[...])
    w = KV_WIDTH
    ksel = _rope(kv[:, 2 * w:3 * w], cos, sin)
    kwin = _rope(kv[:, 4 * w:5 * w], cos, sin)
    rows = jnp.concatenate([kv[:, :2 * w], ksel, kv[:, 3 * w:4 * w]], axis=1)
    win = jnp.concatenate([kwin, kv[:, 5 * w:]], axis=1)
    qg = _dot(hq[...].astype(BF16), wqg_ref[...])
    q = qg[:, :N_HEADS * HEAD_DIM] * (HEAD_DIM ** -0.5)
    qrot = _rope(q, cos, sin)
    gates = _sigmoid(qg[:, N_HEADS * HEAD_DIM:])
    if not transposed:
        for b in range(bt):
            rs = slice(b * t, (b + 1) * t)
            rows_ref[b] = rows[rs]
            win_ref[b] = win[rs]
            kvb_ref[b] = jnp.concatenate([rows[rs, 2 * w:], win[rs]], axis=1).astype(BF16)
            qp_ref[b] = q[rs].astype(BF16)
            qr_ref[b] = qrot[rs].astype(BF16)
            gate_ref[b] = gates[rs]
        return
    rows_ref[0] = rows
    win_ref[0] = win
    kvb_ref[0] = jnp.concatenate([ksel, kwin], axis=1).astype(BF16)
    qp_ref[0] = q.T.astype(BF16)
    qr_ref[0] = qrot.T.astype(BF16)
    gate_ref[0] = gates.T
    ones = jnp.ones((HEAD_DIM, NSA_TK), F32)
    for j in range(t // NSA_TK):
        rs = slice(j * NSA_TK, (j + 1) * NSA_TK)
        vt = jnp.concatenate([kv[rs, 3 * w:4 * w], kv[rs, 5 * w:]], axis=1).T
        parts = []
        for h in range(2 * N_KV_HEADS):
            parts += [vt[h * HEAD_DIM:(h + 1) * HEAD_DIM], ones]
        vt_ref[0, j] = jnp.concatenate(parts, axis=0).astype(BF16)


def _proj(x, mkv, mq, nkv, nq, wkv, wqg, cos, sin, bt, t, transposed):
    bsz, seq, d = x.shape
    t = min(t, seq)
    r = bt * t
    kern = functools.partial(_proj_kernel, bt=bt, t=t, transposed=transposed)
    full = lambda shape: pl.BlockSpec(shape, lambda b, i: (0,) * len(shape))
    tile = lambda n: pl.BlockSpec((bt, t, n), lambda b, i: (b, i, 0))
    tile_t = lambda n: pl.BlockSpec((bt, n, t), lambda b, i: (b, 0, i))
    mod = lambda n: pl.BlockSpec((n, bt, 1, d), lambda b, i: (0, b, 0, 0))
    tab = pl.BlockSpec((r, LANES), lambda b, i: (i, 0))
    nq_cols = N_HEADS * HEAD_DIM
    sds = jax.ShapeDtypeStruct
    out_specs = [tile(4 * KV_WIDTH), tile(2 * KV_WIDTH)]
    out_shape = [sds((bsz, seq, 4 * KV_WIDTH), F32), sds((bsz, seq, 2 * KV_WIDTH), F32)]
    if transposed:
        assert bt == 1 and t % NSA_TK == 0
        vrows = 2 * N_KV_HEADS * LANES
        out_specs += [tile(2 * KV_WIDTH), tile_t(nq_cols), tile_t(nq_cols), tile_t(LANES),
                      pl.BlockSpec((1, t // NSA_TK, vrows, NSA_TK), lambda b, i: (b, i, 0, 0))]
        out_shape += [sds((bsz, seq, 2 * KV_WIDTH), BF16), sds((bsz, nq_cols, seq), BF16),
                      sds((bsz, nq_cols, seq), BF16), sds((bsz, LANES, seq), F32),
                      sds((bsz, seq // NSA_TK, vrows, NSA_TK), BF16)]
    else:
        out_specs += [tile(4 * KV_WIDTH), tile(nq_cols), tile(nq_cols), tile(LANES)]
        out_shape += [sds((bsz, seq, 4 * KV_WIDTH), BF16), sds((bsz, seq, nq_cols), BF16),
                      sds((bsz, seq, nq_cols), BF16), sds((bsz, seq, LANES), F32)]
    return pl.pallas_call(
        kern,
        grid=(bsz // bt, seq // t),
        in_specs=[tile(d), mod(2), mod(3), full((1, d)), full((1, d)), full(wkv.shape), full(wqg.shape),
                  tab, tab],
        out_specs=tuple(out_specs),
        out_shape=tuple(out_shape),
        scratch_shapes=[pltpu.VMEM((r, d), F32), pltpu.VMEM((r, d), F32)],
        compiler_params=_cparams("parallel", "parallel"),
        name="kv_q_proj",
    )(x, mkv, mq, nkv, nq, wkv, wqg, cos, sin)


PAGE_SIZE = 128
CHUNKS_PER_PAGE = PAGE_SIZE // CMP_STRIDE
HEADS_PER_SLAB = LANES // HEAD_DIM
CMP_SLABS = 2 * KV_WIDTH // LANES


def _compress_kernel(tbl_ref, *refs, n_pages, transposed):
    page_refs = refs[:n_pages]
    pe_ref, w1_ref, b1_ref, w2_ref, out_ref, outt_ref, xs = refs[n_pages:]
    m = n_pages * CHUNKS_PER_PAGE
    low = lax.broadcasted_iota(jnp.int32, (m, LANES), 1) < HEAD_DIM
    for k, pr in enumerate(page_refs):
        xs[k * PAGE_SIZE:(k + 1) * PAGE_SIZE, :] = pr[0].T if transposed else pr[0]
    x = [xs[pl.ds(l, m, stride=CMP_STRIDE), :] for l in range(CMP_STRIDE)]
    r = [pltpu.roll(v, HEAD_DIM, 1) for v in x]
    flat = [jnp.concatenate([jnp.where(low, x[l], r[l + 1]) for l in range(0, CMP_STRIDE, 2)], axis=1),
            jnp.concatenate([jnp.where(low, r[l], x[l + 1]) for l in range(0, CMP_STRIDE, 2)], axis=1)]
    w1 = w1_ref[0]
    pb = jnp.dot(pe_ref[0], w1, preferred_element_type=F32, precision=lax.Precision.HIGHEST)
    bias = pb[0:1, :CMP_HIDDEN] + pb[1:2, CMP_HIDDEN:] + b1_ref[0]
    hid = []
    for h in range(HEADS_PER_SLAB):
        pab = _dot(flat[h].astype(BF16), w1.astype(BF16))
        nxt = pltpu.roll(pab[:, CMP_HIDDEN:], m - 1, 0)
        hid.append(_gelu(pab[:, :CMP_HIDDEN] + nxt + bias))
    out = _dot(jnp.concatenate(hid, axis=1).astype(BF16), w2_ref[0])
    out_ref[0] = out.astype(out_ref.dtype)
    outt_ref[0] = out.T.astype(outt_ref.dtype)


def _compress(src, table, n_pages, pe2, w1f, b1, w2bd, transposed):
    bsz = table.shape[0] // n_pages
    m = n_pages * CHUNKS_PER_PAGE
    kern = functools.partial(_compress_kernel, n_pages=n_pages, transposed=transposed)
    slot = lambda s: s // (CMP_SLABS // 2)
    if transposed:
        page = lambda k: pl.BlockSpec((1, LANES, PAGE_SIZE), lambda b, s, tbl: (tbl[b * n_pages + k], s, 0))
    else:
        page = lambda k: pl.BlockSpec((1, PAGE_SIZE, LANES), lambda b, s, tbl: (tbl[b * n_pages + k], 0, s))
    wspec = lambda shape: pl.BlockSpec((1,) + shape, lambda b, s, tbl: (slot(s), 0, 0))
    return pl.pallas_call(
        kern,
        grid_spec=pltpu.PrefetchScalarGridSpec(
            num_scalar_prefetch=1,
            grid=(bsz, CMP_SLABS),
            in_specs=[page(k) for k in range(n_pages)] + [
                wspec(pe2.shape[1:]), wspec(w1f.shape[1:]), wspec(b1.shape[1:]), wspec(w2bd.shape[1:])],
            out_specs=(pl.BlockSpec((1, m, LANES), lambda b, s, tbl: (b, 0, s)),
                       pl.BlockSpec((1, LANES, m), lambda b, s, tbl: (b, s, 0))),
            scratch_shapes=[pltpu.VMEM((n_pages * PAGE_SIZE, LANES), F32)],
        ),
        out_shape=(jax.ShapeDtypeStruct((bsz, m, 2 * KV_WIDTH), BF16),
                   jax.ShapeDtypeStruct((bsz, 2 * KV_WIDTH, m), BF16)),
        compiler_params=_cparams("parallel", "parallel"),
        name="compress_%d" % n_pages,
    )(table, *([src] * n_pages), pe2, w1f, b1, w2bd)


def _sel_matrix(nc, ns, rows, cols):
    r, q = SEL_BLOCK // CMP_STRIDE, CMP_BLOCK // CMP_STRIDE
    offs = (np.arange(r)[:, None] - np.arange(q)[None, :]).reshape(-1)
    target = r * np.arange(ns)[None, :, None] + offs[None, None, :]
    m = (np.arange(nc)[:, None, None] == target).sum(-1).astype(np.float32)
    out = np.zeros((rows, cols), np.float32)
    out[:nc, :ns] = m
    return out


def _select_mask_t(score_t, qpos, ns, n_sel):
    j = lax.broadcasted_iota(jnp.int32, score_t.shape, 0)
    cur = qpos // SEL_BLOCK
    forced = (j == 0) | (j == cur) | (j == cur - 1)
    valid = j * SEL_BLOCK <= qpos
    s = jnp.where(forced, jnp.inf, jnp.where(valid, score_t, -jnp.inf))
    s = jnp.where(j < ns, s, -jnp.inf)
    rank = jnp.zeros(score_t.shape, jnp.int32)
    for jp in range(ns):
        sp = s[jp:jp + 1, :]
        ahead = (sp > s) | ((sp == s) & (jp < j))
        rank = rank + ahead.astype(jnp.int32)
    return ((rank < n_sel) & (j < ns)).astype(F32)


def _nsa_prompt_kernel(x_ref, qpt_ref, qrt_ref, gatet_ref, cmp_ref, cmpt_ref, kk_ref, vt_ref, mod_ref, npost_ref,
                       wot_ref, mselt_ref, expandt_ref, y_ref, bias_sel, bias_win, q4t_s, m_s, a_s, acc_s, s_s, p_s,
                       ot_s, *, tq, tk, seq, ns):
    qt = pl.program_id(1)
    t0 = qt * tq
    w = KV_WIDTH
    grp = GQA_GROUP
    nq = grp * tq
    nc_pad = cmp_ref.shape[1]
    qpos_row = t0 + lax.broadcasted_iota(jnp.int32, (1, tq), 1)
    cends = lax.broadcasted_iota(jnp.int32, (nc_pad, 1), 0) * CMP_STRIDE + (CMP_BLOCK - 1)
    cvalid = cends <= qpos_row
    cbias = jnp.concatenate([jnp.where(cvalid, 0.0, NEG)] * grp, axis=1)
    cmask = jnp.concatenate([cvalid.astype(F32)] * grp, axis=1)
    n_sel_tiles = (t0 + tq + tk - 1) // tk
    n_win_tiles = (WINDOW + tq + tk - 1) // tk

    for i in range(n_win_tiles):
        kpos = t0 + tq - (i + 1) * tk + lax.broadcasted_iota(jnp.int32, (tk, 1), 0)
        diff = qpos_row - kpos
        bias_win[i] = jnp.where((diff >= 0) & (diff < WINDOW) & (kpos >= 0), 0.0, NEG)
    kpos_all = lax.broadcasted_iota(jnp.int32, (seq, 1), 0)

    def attend(kcol, vrow, n_tiles, unroll, start_of, bias_of):
        m_s[...] = jnp.full(m_s.shape, NEG, F32)
        acc_s[...] = jnp.zeros(acc_s.shape, F32)

        def tile(i, u):
            k0 = pl.multiple_of(start_of(i), tk)
            k = kk_ref[0, pl.ds(k0, tk), kcol:kcol + HEAD_DIM]
            vt = vt_ref[0, k0 // tk, vrow:vrow + LANES, :]
            s_s[u] = _dot(k, q4t_s[...])
            for c in range(nq // NSA_COLS):
                cols = slice(c * NSA_COLS, (c + 1) * NSA_COLS)
                s = s_s[u, :, cols] + bias_of(i, k0, (c * NSA_COLS) % tq)
                m_old = m_s[:, cols]
                m_new = jnp.maximum(m_old, jnp.max(s, axis=0, keepdims=True))
                p_s[u, :, cols] = jnp.exp(s - jnp.concatenate([m_new] * (tk // SUBLANES), axis=0)).astype(BF16)
                a_s[u, :, cols] = jnp.exp(m_old - m_new)
                m_s[:, cols] = m_new
            alpha = jnp.concatenate([a_s[u]] * (LANES // SUBLANES), axis=0)
            acc_s[...] = alpha * acc_s[...] + _dot(vt, p_s[u])

        def body(j, carry):
            for u in range(unroll):
                tile(j * unroll + u, u)
            return carry

        lax.fori_loop(0, n_tiles // unroll, body, 0)
        return acc_s[:HEAD_DIM, :] / acc_s[HEAD_DIM:, :]

    for kh in range(N_KV_HEADS):
        heads = [kh * grp + g for g in range(grp)]
        for g, h in enumerate(heads):
            q4t_s[:, g * tq:(g + 1) * tq] = qpt_ref[0, h * HEAD_DIM:(h + 1) * HEAD_DIM, :]
        s = _dot(cmp_ref[0, :, kh * HEAD_DIM:(kh + 1) * HEAD_DIM], q4t_s[...]) + cbias
        e = jnp.exp(s - jnp.max(s, axis=0, keepdims=True))
        p = e / jnp.sum(e, axis=0, keepdims=True) * cmask
        o_cmp = _dot(cmpt_ref[0, w + kh * HEAD_DIM:w + (kh + 1) * HEAD_DIM, :], p.astype(BF16))
        psum = p[:, 0:tq]
        for g in range(1, grp):
            psum = psum + p[:, g * tq:(g + 1) * tq]
        score_t = jnp.dot(mselt_ref[...], psum, preferred_element_type=F32, precision=lax.Precision.HIGHEST)
        sel_t = _select_mask_t(score_t, qpos_row, ns, N_SEL)
        picked = _dot(expandt_ref[...], sel_t.astype(BF16))
        bias_sel[...] = jnp.where((picked > 0.5) & (kpos_all <= qpos_row), 0.0, NEG)

        for g, h in enumerate(heads):
            q4t_s[:, g * tq:(g + 1) * tq] = qrt_ref[0, h * HEAD_DIM:(h + 1) * HEAD_DIM, :]
        o_sel = attend(kh * HEAD_DIM, kh * LANES, n_sel_tiles, tq // tk, lambda i: i * tk,
                       lambda i, k0, c0: bias_sel[pl.ds(k0, tk), c0:c0 + NSA_COLS])
        o_win = attend(w + kh * HEAD_DIM, (N_KV_HEADS + kh) * LANES, n_win_tiles, n_win_tiles,
                       lambda i: jnp.maximum(t0 + tq - (i + 1) * tk, 0),
                       lambda i, k0, c0: bias_win[i, :, c0:c0 + NSA_COLS])
        for g, h in enumerate(heads):
            cs = slice(g * tq, (g + 1) * tq)
            gc = gatet_ref[0, h:h + 1, :]
            gs = gatet_ref[0, N_HEADS + h:N_HEADS + h + 1, :]
            gw = gatet_ref[0, 2 * N_HEADS + h:2 * N_HEADS + h + 1, :]
            ot_s[h * HEAD_DIM:(h + 1) * HEAD_DIM, :] = (
                gc * o_cmp[:, cs] + gs * o_sel[:, cs] + gw * o_win[:, cs]).astype(BF16)

    out = _dot(wot_ref[...], ot_s[...]).T
    y_ref[0] = x_ref[0] + mod_ref[0] * _rms(out, npost_ref[...])


def _nsa_prompt(x, qpt, qrt, gatet, cmp, cmpt, kk, vt, gate_mod, npost, wot, tq, tk):
    bsz, seq, d = x.shape
    tq = min(tq, seq)
    assert tq % tk == 0 and seq % tq == 0 and tq % NSA_COLS == 0 and vt.shape[3] == tk
    nc = (seq - CMP_BLOCK) // CMP_STRIDE + 1
    ns = -(-seq // SEL_BLOCK)
    nc_pad = cmp.shape[1]
    j_pad = -(-ns // 16) * 16
    n_win = (WINDOW + tq + tk - 1) // tk
    mselt = jnp.asarray(_sel_matrix(nc, ns, nc_pad, j_pad).T)
    expandt = jnp.asarray((np.arange(seq)[:, None] // SEL_BLOCK == np.arange(j_pad)[None, :]), BF16)
    kern = functools.partial(_nsa_prompt_kernel, tq=tq, tk=tk, seq=seq, ns=ns)
    full = lambda shape: pl.BlockSpec(shape, lambda b, i: (0,) * len(shape))
    tile = lambda n: pl.BlockSpec((1, tq, n), lambda b, i: (b, i, 0))
    tile_t = lambda n: pl.BlockSpec((1, n, tq), lambda b, i: (b, 0, i))
    per_b = lambda *shape: pl.BlockSpec((1,) + shape, lambda b, i: (b,) + (0,) * len(shape))
    nq = GQA_GROUP * tq
    return pl.pallas_call(
        kern,
        grid=(bsz, seq // tq),
        in_specs=[tile(d), tile_t(qpt.shape[1]), tile_t(qrt.shape[1]), tile_t(LANES), per_b(*cmp.shape[1:]),
                  per_b(*cmpt.shape[1:]), per_b(*kk.shape[1:]), per_b(*vt.shape[1:]), per_b(1, d), full((1, d)),
                  full(wot.shape), full(mselt.shape), full(expandt.shape)],
        out_specs=tile(d),
        out_shape=jax.ShapeDtypeStruct((bsz, seq, d), F32),
        scratch_shapes=[pltpu.VMEM((seq, tq), F32), pltpu.VMEM((n_win, tk, tq), F32),
                        pltpu.VMEM((HEAD_DIM, nq), BF16), pltpu.VMEM((SUBLANES, nq), F32),
                        pltpu.VMEM((n_win, SUBLANES, nq), F32), pltpu.VMEM((LANES, nq), F32),
                        pltpu.VMEM((n_win, tk, nq), F32), pltpu.VMEM((n_win, tk, nq), BF16),
                        pltpu.VMEM((d, tq), BF16)],
        compiler_params=_cparams("parallel", "arbitrary"),
        name="nsa_prompt",
    )(x, qpt, qrt, gatet, cmp, cmpt, kk, vt, gate_mod, npost, wot, mselt, expandt)


def _nsa_sample_kernel(tbl_ref, *refs, n_pages, pages_per_step, past_len, n_steps, ns):
    page_refs = refs[:pages_per_step]
    (qp_ref, qr_ref, gate_ref, cmp_ref, new_ref, wc_ref, mselt_ref, expand_ref, expand_new_ref, out_ref,
     selt, m_s, l_s, acc_s, ocmp_s) = refs[pages_per_step:]
    pc = pl.program_id(1)
    w = KV_WIDTH
    nrow = qp_ref.shape[1]
    step_of_row = lax.broadcasted_iota(jnp.int32, (nrow, 1), 0) % n_steps
    qpos_col = past_len + step_of_row
    qr = qr_ref[0]

    @pl.when(pc == 0)
    def _():
        nc_pad = cmp_ref.shape[1]
        cends = lax.broadcasted_iota(jnp.int32, (1, nc_pad), 1) * CMP_STRIDE + (CMP_BLOCK - 1)
        cvalid = cends <= qpos_col
        s = jnp.where(cvalid, _dot_nt(qp_ref[0], cmp_ref[0, :, :w]), NEG)
        e = jnp.exp(s - jnp.max(s, axis=-1, keepdims=True))
        p = jnp.where(cvalid, e / jnp.sum(e, axis=-1, keepdims=True), 0.0)
        ocmp_s[...] = _dot(p.astype(BF16), cmp_ref[0, :, w:])
        grp = nrow // GQA_GROUP
        psum = p[0:grp]
        for g in range(1, GQA_GROUP):
            psum = psum + p[g * grp:(g + 1) * grp]
        psum = jnp.concatenate([psum] * GQA_GROUP, axis=0)
        score_t = lax.dot_general(mselt_ref[...], psum, (((1,), (1,)), ((), ())),
                                  preferred_element_type=F32, precision=lax.Precision.HIGHEST)
        qpos_row = past_len + lax.broadcasted_iota(jnp.int32, (1, nrow), 1) % n_steps
        selt[...] = _select_mask_t(score_t, qpos_row, ns, N_SEL).astype(BF16)
        m_s[...] = jnp.full(m_s.shape, NEG, F32)
        l_s[...] = jnp.zeros(l_s.shape, F32)
        acc_s[...] = jnp.zeros(acc_s.shape, F32)

    def flash(s, allowed, pv):
        s = jnp.where(allowed, s, NEG)
        m_old = m_s[...]
        m_new = jnp.maximum(m_old, jnp.max(s, axis=-1, keepdims=True))
        alpha = jnp.exp(m_old - m_new)
        p = jnp.exp(s - m_new)
        l_s[...] = alpha * l_s[...] + jnp.sum(p, axis=-1, keepdims=True)
        acc_s[...] = alpha * acc_s[...] + pv(p.astype(BF16))
        m_s[...] = m_new

    kt = jnp.concatenate([pr[0, :w, :] for pr in page_refs], axis=1).astype(BF16)
    vt = jnp.concatenate([pr[0, w:, :] for pr in page_refs], axis=1).astype(BF16)
    allowed = lax.dot_general(selt[...], expand_ref[...], (((0,), (0,)), ((), ())),
                              preferred_element_type=F32) > 0.5
    flash(_dot(qr, kt), allowed, lambda p: _dot_nt(p, vt))

    @pl.when(pc == pl.num_programs(1) - 1)
    def _():
        pad = jnp.zeros((LANES - n_steps, 4 * w), F32)
        new = jnp.concatenate([new_ref[0].astype(F32), pad], axis=0).astype(BF16)
        key_step = lax.broadcasted_iota(jnp.int32, (1, LANES), 1)
        causal_new = key_step <= step_of_row
        picked = lax.dot_general(selt[...], expand_new_ref[...], (((0,), (0,)), ((), ())),
                                 preferred_element_type=F32) > 0.5
        flash(_dot_nt(qr, new[:, :w]), causal_new & picked, lambda p: _dot(p, new[:, w:2 * w]))
        o_sel = acc_s[...] / l_s[...]

        w_buf = wc_ref.shape[2]
        buf_idx = lax.broadcasted_iota(jnp.int32, (1, w_buf), 1)
        diff_buf = step_of_row + w_buf - buf_idx
        s_buf = jnp.where(diff_buf < WINDOW, _dot(qr, wc_ref[0, :w, :].astype(BF16)), NEG)
        s_new = jnp.where(causal_new, _dot_nt(qr, new[:, 2 * w:3 * w]), NEG)
        sw = jnp.concatenate([s_buf, s_new], axis=1)
        ew = jnp.exp(sw - jnp.max(sw, axis=-1, keepdims=True))
        pw = (ew / jnp.sum(ew, axis=-1, keepdims=True)).astype(BF16)
        o_win = _dot_nt(pw[:, :w_buf], wc_ref[0, w:, :].astype(BF16)) + _dot(pw[:, w_buf:], new[:, 3 * w:])
        g = gate_ref[0]
        out_ref[0] = g[:, 0:1] * ocmp_s[...] + g[:, 1:2] * o_sel + g[:, 2:3] * o_win


SAMPLE_PAGES_PER_STEP = 16


def _nsa_sample(pool, table, n_pages, qp_bd, qr_bd, gates_r, cmp, new_kvb, wcache, past_len):
    bsz, nrow, _ = qp_bd.shape
    n_steps = new_kvb.shape[1]
    pps = min(SAMPLE_PAGES_PER_STEP, n_pages)
    nc = (past_len + n_steps - CMP_BLOCK) // CMP_STRIDE + 1
    ns = -(-(past_len + n_steps) // SEL_BLOCK)
    assert ns == past_len // SEL_BLOCK + 1 and n_steps <= SEL_BLOCK
    nc_pad = cmp.shape[1]
    j_pad = -(-ns // 16) * 16
    mselt = jnp.asarray(_sel_matrix(nc, ns, nc_pad, j_pad).T)
    expand = jnp.asarray((np.arange(j_pad)[:, None] == np.arange(past_len)[None, :] // SEL_BLOCK), BF16)
    expand_new = jnp.asarray(np.broadcast_to(np.arange(j_pad)[:, None] == ns - 1, (j_pad, LANES)), BF16)
    kern = functools.partial(_nsa_sample_kernel, n_pages=n_pages, pages_per_step=pps, past_len=past_len,
                             n_steps=n_steps, ns=ns)
    sel_rows = 1
    page = lambda k: pl.BlockSpec((1, 2 * KV_WIDTH, PAGE_SIZE),
                                  lambda b, c, tbl: (tbl[b * n_pages + c * pps + k], sel_rows, 0))
    per_b = lambda r, n: pl.BlockSpec((1, r, n), lambda b, c, tbl: (b, 0, 0))
    full = lambda shape: pl.BlockSpec(shape, lambda b, c, tbl: (0,) * len(shape))
    return pl.pallas_call(
        kern,
        grid_spec=pltpu.PrefetchScalarGridSpec(
            num_scalar_prefetch=1,
            grid=(bsz, n_pages // pps),
            in_specs=[page(k) for k in range(pps)] + [
                per_b(nrow, KV_WIDTH), per_b(nrow, KV_WIDTH), per_b(nrow, LANES), per_b(nc_pad, 2 * KV_WIDTH),
                per_b(n_steps, 4 * KV_WIDTH), per_b(2 * KV_WIDTH, wcache.shape[2]), full(mselt.shape),
                pl.BlockSpec((j_pad, pps * PAGE_SIZE), lambda b, c, tbl: (0, c)), full(expand_new.shape)],
            out_specs=per_b(nrow, KV_WIDTH),
            scratch_shapes=[pltpu.VMEM((j_pad, nrow), BF16), pltpu.VMEM((nrow, 1), F32),
                            pltpu.VMEM((nrow, 1), F32), pltpu.VMEM((nrow, KV_WIDTH), F32),
                            pltpu.VMEM((nrow, KV_WIDTH), F32)],
        ),
        out_shape=jax.ShapeDtypeStruct((bsz, nrow, KV_WIDTH), F32),
        compiler_params=_cparams("parallel", "arbitrary"),
        name="nsa_sample",
    )(table, *([pool] * pps), qp_bd, qr_bd, gates_r, cmp, new_kvb, wcache, mselt, expand, expand_new)


def _oproj_kernel(o_ref, x_ref, gate_ref, npost_ref, wo_ref, y_ref):
    out = _dot(o_ref[...].astype(BF16), wo_ref[...])
    y_ref[...] = x_ref[...] + gate_ref[...] * _rms(out, npost_ref[...])


def _oproj(o, x, gate, npost, wo):
    return pl.pallas_call(
        _oproj_kernel,
        out_shape=jax.ShapeDtypeStruct(x.shape, F32),
        compiler_params=pltpu.CompilerParams(vmem_limit_bytes=VMEM_LIMIT),
        name="nsa_out_proj",
    )(o, x, gate, npost, wo)


S5_T = 64
FFN_T = 64
FFN_TN = D_FF // 2


def _prepare(p):
    w = {}
    a_re, a_im, zb_re, zb_im = _s5_prep(p['ssm_lam_re'][0], p['ssm_lam_im'][0], p['ssm_log_dt'][0],
                                        p['ssm_b_re'][0], p['ssm_b_im'][0])
    w['s5_a'] = jnp.stack([a_re.reshape(-1), a_im.reshape(-1)])
    w['s5_wb'] = jnp.concatenate([_s5_block_diag_in(zb_re), _s5_block_diag_in(zb_im)], axis=-1).astype(BF16)
    w['s5_wcre'] = _s5_block_diag_out(p['ssm_c_re'][0]).astype(BF16)
    w['s5_wcim'] = _s5_block_diag_out(p['ssm_c_im'][0]).astype(BF16)
    w['w_glu'] = p['w_glu'][0].astype(BF16)
    w['ffn_wa'] = p['ffn_w_up'][:, :, :D_FF].astype(BF16)
    w['ffn_wg'] = p['ffn_w_up'][:, :, D_FF:].astype(BF16)
    w['ffn_wd'] = p['ffn_w_down'].astype(BF16)
    w['w_kv'] = p['w_kv'].astype(BF16)
    w['w_qg'] = jnp.pad(p['w_qg'][0], ((0, 0), (0, QG_PAD - p['w_qg'].shape[2]))).astype(BF16)
    w['w_o'] = p['w_o'][0].astype(BF16)
    w['w_o_t'] = p['w_o'][0].T.astype(BF16)
    half = CMP_BLOCK // 2
    w1r = p['cmp_w1'].reshape(2, CMP_BLOCK, HEAD_DIM, CMP_HIDDEN)
    w['cmp_w1'] = jnp.concatenate([w1r[:, :half], w1r[:, half:]], axis=-1).reshape(
        2, half * HEAD_DIM, 2 * CMP_HIDDEN)
    w['cmp_pe'] = p['cmp_pe'].reshape(2, 2, half * HEAD_DIM)
    w['cmp_b1'] = p['cmp_b1'].reshape(2, 1, CMP_HIDDEN)
    eye = jnp.eye(HEADS_PER_SLAB, dtype=F32)
    w['cmp_w2'] = jnp.einsum('shd,ab->sahbd', p['cmp_w2'], eye).reshape(
        2, HEADS_PER_SLAB * CMP_HIDDEN, LANES).astype(BF16)
    return w


def _split3(m):
    return m.reshape(m.shape[0], 3, D_MODEL).transpose(1, 0, 2)


def _ffn_call(x, mod, layer, p, w, prefix):
    bsz = x.shape[0]
    halo = (CONV_WIDTH - 1) * bsz
    pf = prefix.transpose(1, 0, 2).reshape(halo, 2 * D_FF)
    cw, cb = p['ffn_conv_w'][layer], p['ffn_conv_b'][layer].reshape(1, -1)
    y, sta, stg = _ffn_layer(x, mod, p['norm_pre'][layer, 1:2], p['norm_post'][layer, 1:2],
                             w['ffn_wa'][layer], w['ffn_wg'][layer], cw[:, :D_FF], cw[:, D_FF:],
                             cb[:, :D_FF], cb[:, D_FF:], w['ffn_wd'][layer], pf[:, :D_FF], pf[:, D_FF:],
                             FFN_T, FFN_TN)
    st = jnp.stack([sta, stg]).reshape(2, D_FF // FFN_TN, CONV_WIDTH - 1, bsz, FFN_TN)
    st = st.transpose(3, 2, 0, 1, 4).reshape(bsz, CONV_WIDTH - 1, 2 * D_FF)
    return y, st


def _layer0(x, mods, p, w, h0, prefix):
    bsz = x.shape[0]
    y, s = _s5_layer(x, _split3(mods[0]), p['norm_pre'][0, 0:1], p['norm_post'][0, 0:1], w['s5_wb'],
                     w['s5_wcre'], w['s5_wcim'], w['s5_a'], p['ssm_d'][0:1], w['w_glu'], p['b_glu'][0:1],
                     h0, S5_T)
    s = s.reshape(2, bsz, N_GROUPS, SSM_STATE)
    y, st = _ffn_call(y, _split3(mods[1]), 0, p, w, prefix)
    return y, s[0], s[1], st


PROJ_T = 512


def _proj_call(x, mods, modkv, p, w, pos, bt):
    bsz, seq, _ = x.shape
    cos, sin = _rope_tables(pos)
    if bt > 1:
        cos, sin = jnp.tile(cos, (bt, 1)), jnp.tile(sin, (bt, 1))
    mkv = modkv.reshape(bsz, 2, 1, D_MODEL).transpose(1, 0, 2, 3)
    mq = _split3(mods[2]).reshape(3, bsz, 1, D_MODEL)
    return _proj(x, mkv, mq, p['norm_kv'].reshape(1, D_MODEL), p['norm_pre'][1, 0:1], w['w_kv'], w['w_qg'],
                 cos, sin, bt, PROJ_T, bt == 1)


def _prompt_group(x, mods, modkv, p, w):
    bsz, seq, _ = x.shape
    zeros_state = jnp.zeros((2, bsz, N_STATE), F32)
    zeros_prefix = jnp.zeros((bsz, CONV_WIDTH - 1, 2 * D_FF), F32)
    x1, s_re, s_im, conv0 = _layer0(x, mods, p, w, zeros_state, zeros_prefix)
    rows, win, kk, qpt, qrt, gatet, vt = _proj_call(x1, mods, modkv, p, w, np.arange(seq), 1)
    n_pages = seq // PAGE_SIZE
    table = jnp.arange(bsz * n_pages, dtype=jnp.int32)
    cmp, cmpt = _compress(rows.reshape(bsz * n_pages, PAGE_SIZE, 4 * KV_WIDTH), table, n_pages,
                          w['cmp_pe'], w['cmp_w1'], w['cmp_b1'], w['cmp_w2'], False)
    gate_mod = _split3(mods[2])[2].reshape(bsz, 1, D_MODEL)
    x2 = _nsa_prompt(x1, qpt, qrt, gatet, cmp, cmpt, kk, vt, gate_mod, p['norm_post'][1, 0:1], w['w_o_t'],
                     NSA_TQ, NSA_TK)
    y, conv1 = _ffn_call(x2, _split3(mods[3]), 1, p, w, zeros_prefix)
    kvd = (N_KV_HEADS, HEAD_DIM)
    return (y, rows.reshape(bsz, seq, 4, *kvd), win[:, -min(WINDOW, seq):].reshape(bsz, -1, 2, *kvd),
            s_re[None], s_im[None], jnp.stack([conv0, conv1]))


def _block_diag_rows(q):
    bsz, steps, _ = q.shape
    q5 = q.reshape(bsz, steps, N_KV_HEADS, GQA_GROUP, HEAD_DIM)
    eye = jnp.eye(N_KV_HEADS, dtype=q.dtype)
    return jnp.einsum('bqkgd,kj->bgkqjd', q5, eye).reshape(bsz, GQA_GROUP * N_KV_HEADS * steps, KV_WIDTH)


def _sample_group(x, mods, modkv, p, w, pool, page_table, wcache, ssm_re, ssm_im, conv_state):
    bsz, steps, _ = x.shape
    n_pages = page_table.shape[1]
    past_len = n_pages * PAGE_SIZE
    h0 = jnp.stack([ssm_re[0].reshape(bsz, N_STATE), ssm_im[0].reshape(bsz, N_STATE)])
    x1, s_re, s_im, conv0 = _layer0(x, mods, p, w, h0, conv_state[0])
    rows, win, kvb, qp, qr, gates = _proj_call(x1, mods, modkv, p, w, past_len + np.arange(steps), bsz)
    pool2 = pool.transpose(0, 2, 3, 4, 1).reshape(pool.shape[0], 4 * KV_WIDTH, PAGE_SIZE)
    wcache2 = wcache.transpose(0, 2, 3, 4, 1).reshape(bsz, 2 * KV_WIDTH, wcache.shape[1])
    table = page_table.reshape(-1)
    cmp, _ = _compress(pool2, table, n_pages, w['cmp_pe'], w['cmp_w1'], w['cmp_b1'], w['cmp_w2'], True)
    nrow = GQA_GROUP * N_KV_HEADS * steps
    g5 = gates[:, :, :3 * N_HEADS].reshape(bsz, steps, 3, N_KV_HEADS, GQA_GROUP)
    gates_r = jnp.pad(g5.transpose(0, 4, 3, 1, 2).reshape(bsz, nrow, 3), ((0, 0), (0, 0), (0, LANES - 3)))
    o_bd = _nsa_sample(pool2, table, n_pages, _block_diag_rows(qp), _block_diag_rows(qr), gates_r, cmp, kvb,
                       wcache2, past_len)
    o6 = o_bd.reshape(bsz, GQA_GROUP, N_KV_HEADS, steps, N_KV_HEADS, HEAD_DIM)
    o = jnp.einsum('bgkqjd,kj->bqkgd', o6, jnp.eye(N_KV_HEADS, dtype=F32)).reshape(bsz * steps, D_MODEL)
    gate_mod = jnp.repeat(_split3(mods[2])[2], steps, axis=0)
    x2 = _oproj(o, x1.reshape(bsz * steps, D_MODEL), gate_mod, p['norm_post'][1, 0:1], w['w_o'])
    y, conv1 = _ffn_call(x2.reshape(bsz, steps, D_MODEL), _split3(mods[3]), 1, p, w, conv_state[1])
    kvd = (N_KV_HEADS, HEAD_DIM)
    return (y, rows.reshape(bsz, steps, 4, *kvd), win.reshape(bsz, steps, 2, *kvd),
            s_re[None], s_im[None], jnp.stack([conv0, conv1]))


def kernel(x_prompt, x_sample, cache_nsa_paged, cache_nsa_window, state_ssm_re, state_ssm_im, state_ffn_conv,
           page_table, c_prompt, c_sample, norm_pre, norm_post, w_mod, b_mod, ssm_lam_re, ssm_lam_im,
           ssm_log_dt, ssm_b_re, ssm_b_im, ssm_c_re, ssm_c_im, ssm_d, w_glu, b_glu, ffn_w_up, ffn_conv_w,
           ffn_conv_b, ffn_w_down, norm_kv, w_mod_kv, b_mod_kv, w_kv, cmp_pe, cmp_w1, cmp_b1, cmp_w2,
           w_qg, w_o):
    p = {'norm_pre': norm_pre, 'norm_post': norm_post, 'w_mod': w_mod, 'b_mod': b_mod,
         'ssm_lam_re': ssm_lam_re, 'ssm_lam_im': ssm_lam_im, 'ssm_log_dt': ssm_log_dt,
         'ssm_b_re': ssm_b_re, 'ssm_b_im': ssm_b_im, 'ssm_c_re': ssm_c_re, 'ssm_c_im': ssm_c_im,
         'ssm_d': ssm_d, 'w_glu': w_glu, 'b_glu': b_glu, 'ffn_w_up': ffn_w_up, 'ffn_conv_w': ffn_conv_w,
         'ffn_conv_b': ffn_conv_b, 'ffn_w_down': ffn_w_down, 'norm_kv': norm_kv, 'w_mod_kv': w_mod_kv,
         'b_mod_kv': b_mod_kv, 'w_kv': w_kv, 'cmp_pe': cmp_pe, 'cmp_w1': cmp_w1, 'cmp_b1': cmp_b1,
         'cmp_w2': cmp_w2, 'w_qg': w_qg, 'w_o': w_o}
    w = _prepare(p)
    nb, ns = x_prompt.shape[0], x_sample.shape[0]
    c_all = jnp.concatenate([c_prompt, c_sample])
    mods = _mods(c_all, w_mod.reshape(4, D_MODEL, 3 * D_MODEL), b_mod.reshape(4, 3 * D_MODEL))
    modkv = _mods(c_all, w_mod_kv[None], b_mod_kv[None])[0]
    y_p, rows_p, win_p, sre_p, sim_p, conv_p = _prompt_group(x_prompt, mods[:, :nb], modkv[:nb], p, w)
    y_s, rows_s, win_s, sre_s, sim_s, conv_s = _sample_group(
        x_sample, mods[:, nb:], modkv[nb:], p, w, cache_nsa_paged, page_table, cache_nsa_window,
        state_ssm_re, state_ssm_im, state_ffn_conv)
    return (y_p, y_s, rows_p, rows_s, win_p, win_s, sre_p, sim_p, sre_s, sim_s, conv_p, conv_s)
```

```python
import functools
import math

import numpy as np
import jax
import jax.numpy as jnp
from jax import lax
from jax.experimental import pallas as pl
from jax.experimental.pallas import tpu as pltpu

D_MODEL = 1024
SSM_GROUP = 16
N_GROUPS = D_MODEL // SSM_GROUP
SSM_STATE = 64
N_STATE = N_GROUPS * SSM_STATE
N_HEADS = 16
HEAD_DIM = 64
N_KV_HEADS = 4
GQA_GROUP = N_HEADS // N_KV_HEADS
KV_WIDTH = N_KV_HEADS * HEAD_DIM
CMP_BLOCK = 32
CMP_STRIDE = 16
CMP_HIDDEN = 2 * HEAD_DIM
SEL_BLOCK = 64
N_SEL = 16
WINDOW = 512
ROPE_THETA = 10000.0
D_FF = 2816
CONV_WIDTH = 3
EPS = 1e-6
NEG = -1e30
LOG2E = math.log2(math.e)

SUBLANES = 8
LANES = 128
MXU_DIM = 256
VMEM_LIMIT = 56 * 1024 * 1024

F32 = jnp.float32
BF16 = jnp.bfloat16


def _cparams(*sem):
    return pltpu.CompilerParams(dimension_semantics=sem, vmem_limit_bytes=VMEM_LIMIT)


def _sigmoid(x):
    return 1.0 / (1.0 + jnp.exp(-x))


def _silu(x):
    return x * _sigmoid(x)


def _gelu(x):
    return 0.5 * x * (1.0 + jnp.tanh(math.sqrt(2.0 / math.pi) * (x + 0.044715 * (x * x * x))))


def _rms(x, g):
    return x * lax.rsqrt(jnp.mean(x * x, axis=-1, keepdims=True) + EPS) * g


def _dot(a, b):
    return jnp.dot(a, b, preferred_element_type=F32)


def _dot_nt(a, b):
    return lax.dot_general(a, b, (((1,), (1,)), ((), ())), preferred_element_type=F32)


def _mods_kernel(c_ref, w_ref, b_ref, o_ref):
    c = _silu(c_ref[...])
    o_ref[0] = jnp.dot(c, w_ref[0], preferred_element_type=F32,
                       precision=lax.Precision.HIGHEST) + b_ref[0]


def _mods(c, w, b):
    s, d, n = w.shape
    m = c.shape[0]
    tn = 1024
    return pl.pallas_call(
        _mods_kernel,
        grid=(s, n // tn),
        in_specs=[pl.BlockSpec((m, d), lambda i, j: (0, 0)),
                  pl.BlockSpec((1, d, tn), lambda i, j: (i, 0, j)),
                  pl.BlockSpec((1, 1, tn), lambda i, j: (i, 0, j))],
        out_specs=pl.BlockSpec((1, m, tn), lambda i, j: (i, 0, j)),
        out_shape=jax.ShapeDtypeStruct((s, m, n), F32),
        compiler_params=_cparams("parallel", "parallel"),
        name="adaln_mods",
    )(c, w, b.reshape(s, 1, n))


def _s5_prep_kernel(lr_ref, li_ref, ldt_ref, bre_ref, bim_ref, are_ref, aim_ref, zbre_ref, zbim_ref):
    lr, li = lr_ref[...], li_ref[...]
    dt = jnp.exp(ldt_ref[...])
    mag = jnp.exp(lr * dt)
    a_re, a_im = mag * jnp.cos(li * dt), mag * jnp.sin(li * dt)
    den = lr * lr + li * li
    ir, ii = lr / den, -li / den
    z_re = (a_re - 1.0) * ir - a_im * ii
    z_im = (a_re - 1.0) * ii + a_im * ir
    are_ref[...] = a_re
    aim_ref[...] = a_im
    bre, bim = bre_ref[...], bim_ref[...]
    zr, zi = z_re[:, None, :], z_im[:, None, :]
    zbre_ref[...] = zr * bre - zi * bim
    zbim_ref[...] = zr * bim + zi * bre


def _s5_prep(lam_re, lam_im, log_dt, b_re, b_im):
    g, p, c = b_re.shape
    return pl.pallas_call(
        _s5_prep_kernel,
        out_shape=(jax.ShapeDtypeStruct((g, p), F32), jax.ShapeDtypeStruct((g, p), F32),
                   jax.ShapeDtypeStruct((g, c, p), F32), jax.ShapeDtypeStruct((g, c, p), F32)),
        name="s5_prep",
    )(lam_re, lam_im, log_dt.reshape(g, 1), b_re.transpose(0, 2, 1), b_im.transpose(0, 2, 1))


S5_KBLOCKS = D_MODEL // MXU_DIM
S5_GPB = MXU_DIM // SSM_GROUP
S5_SPB = S5_GPB * SSM_STATE


def _s5_block_diag_in(zb):
    z = zb.reshape(S5_KBLOCKS, S5_GPB, SSM_GROUP, SSM_STATE)
    eye = jnp.eye(S5_GPB, dtype=zb.dtype)
    return jnp.einsum('kgcp,gh->kgchp', z, eye).reshape(S5_KBLOCKS, MXU_DIM, S5_SPB)


def _s5_block_diag_out(cm):
    z = cm.reshape(S5_KBLOCKS, S5_GPB, SSM_GROUP, SSM_STATE)
    eye = jnp.eye(S5_GPB, dtype=cm.dtype)
    return jnp.einsum('kgcp,gh->kgphc', z, eye).reshape(S5_KBLOCKS, S5_SPB, MXU_DIM)


S5_LANE_BLOCKS = 4


def _to_time_major(dst, b, val, t, bt):
    for c in range(val.shape[1] // LANES):
        dst[c, pl.ds(b, t, stride=bt), :] = val[:, c * LANES:(c + 1) * LANES]


def _from_time_major(src, b, t, bt):
    return jnp.concatenate([src[c, pl.ds(b, t, stride=bt), :] for c in range(src.shape[0])], axis=1)


def _slab_cols(src, c0, c1):
    return jnp.concatenate([src[c] for c in range(c0 // LANES, c1 // LANES)], axis=1)


def _s5_kernel(x_ref, mod_ref, npre_ref, npost_ref, wb_ref, wcre_ref, wcim_ref, a_ref, d_ref,
               wglu_ref, bglu_ref, h0_ref, y_ref, sout_ref, htm, bu, ytm, otm, state, *, bt, t):
    nsub = bt // SUBLANES

    @pl.when(pl.program_id(0) == 0)
    def _():
        state[...] = h0_ref[...]

    for b in range(bt):
        hb = _rms(x_ref[b], npre_ref[...]) * (1.0 + mod_ref[1, b:b + 1, :]) + mod_ref[0, b:b + 1, :]
        _to_time_major(htm, b, hb, t, bt)

    for kb in range(S5_KBLOCKS):
        cs = slice(kb * MXU_DIM, (kb + 1) * MXU_DIM)
        hkb = _slab_cols(htm, kb * MXU_DIM, (kb + 1) * MXU_DIM)
        bu[...] = _dot(hkb.astype(BF16), wb_ref[kb])
        for lb0 in range(0, S5_SPB // LANES, S5_LANE_BLOCKS):
            for sb in range(nsub):
                cols = [(lb0 + i) * LANES for i in range(S5_LANE_BLOCKS)]
                rs = slice(sb * SUBLANES, (sb + 1) * SUBLANES)
                ar = [jnp.broadcast_to(a_ref[0:1, kb * S5_SPB + c:kb * S5_SPB + c + LANES], (SUBLANES, LANES))
                      for c in cols]
                ai = [jnp.broadcast_to(a_ref[1:2, kb * S5_SPB + c:kb * S5_SPB + c + LANES], (SUBLANES, LANES))
                      for c in cols]
                init = tuple(state[0, rs, kb * S5_SPB + c:kb * S5_SPB + c + LANES] for c in cols) + \
                    tuple(state[1, rs, kb * S5_SPB + c:kb * S5_SPB + c + LANES] for c in cols)

                def step(i, carry, cols=cols, ar=ar, ai=ai, sb=sb):
                    r0 = pl.multiple_of(i * bt + sb * SUBLANES, SUBLANES)
                    out = [None] * (2 * S5_LANE_BLOCKS)
                    for n, c in enumerate(cols):
                        sr, si = carry[n], carry[S5_LANE_BLOCKS + n]
                        nr = ar[n] * sr - ai[n] * si + bu[pl.ds(r0, SUBLANES), c:c + LANES]
                        ni = ar[n] * si + ai[n] * sr + bu[pl.ds(r0, SUBLANES), S5_SPB + c:S5_SPB + c + LANES]
                        bu[pl.ds(r0, SUBLANES), c:c + LANES] = nr
                        bu[pl.ds(r0, SUBLANES), S5_SPB + c:S5_SPB + c + LANES] = ni
                        out[n], out[S5_LANE_BLOCKS + n] = nr, ni
                    return tuple(out)

                fin = lax.fori_loop(0, t, step, init, unroll=min(t, 8))
                for n, c in enumerate(cols):
                    state[0, rs, kb * S5_SPB + c:kb * S5_SPB + c + LANES] = fin[n]
                    state[1, rs, kb * S5_SPB + c:kb * S5_SPB + c + LANES] = fin[S5_LANE_BLOCKS + n]
        ykb = _dot(bu[:, :S5_SPB].astype(BF16), wcre_ref[kb]) - _dot(bu[:, S5_SPB:].astype(BF16), wcim_ref[kb])
        ytm[:, cs] = ykb + d_ref[:, cs] * hkb

    g = _gelu(ytm[...])
    out = g * _sigmoid(_dot(g.astype(BF16), wglu_ref[...]) + bglu_ref[...])
    out = _rms(out, npost_ref[...])
    for c in range(otm.shape[0]):
        otm[c] = out[:, c * LANES:(c + 1) * LANES]
    for b in range(bt):
        y_ref[b] = x_ref[b] + mod_ref[2, b:b + 1, :] * _from_time_major(otm, b, t, bt)
    sout_ref[...] = state[...]


def _s5_layer(x, mod, npre, npost, wb, wcre, wcim, a, dskip, wglu, bglu, h0, t):
    bsz, seq, d = x.shape
    t = min(t, seq)
    rows = bsz * t
    kern = functools.partial(_s5_kernel, bt=bsz, t=t)
    full = lambda shape: pl.BlockSpec(shape, lambda i: (0,) * len(shape))
    return pl.pallas_call(
        kern,
        grid=(seq // t,),
        in_specs=[pl.BlockSpec((bsz, t, d), lambda i: (0, i, 0)),
                  full((3, bsz, d)), full((1, d)), full((1, d)),
                  full(wb.shape), full(wcre.shape), full(wcim.shape), full((2, N_STATE)), full((1, d)),
                  full((d, d)), full((1, d)), full((2, bsz, N_STATE))],
        out_specs=(pl.BlockSpec((bsz, t, d), lambda i: (0, i, 0)), full((2, bsz, N_STATE))),
        out_shape=(jax.ShapeDtypeStruct((bsz, seq, d), F32), jax.ShapeDtypeStruct((2, bsz, N_STATE), F32)),
        scratch_shapes=[pltpu.VMEM((d // LANES, rows, LANES), F32), pltpu.VMEM((rows, 2 * S5_SPB), F32),
                        pltpu.VMEM((rows, d), F32), pltpu.VMEM((d // LANES, rows, LANES), F32),
                        pltpu.VMEM((2, bsz, N_STATE), F32)],
        compiler_params=_cparams("arbitrary"),
        name="s5_layer",
    )(x, mod, npre, npost, wb, wcre, wcim, a, dskip, wglu, bglu, h0)


def _ffn_kernel(x_ref, mod_ref, npre_ref, npost_ref, wa_ref, wg_ref, cwa_ref, cwg_ref, cba_ref, cbg_ref,
                wd_ref, pfa_ref, pfg_ref, y_ref, sta_ref, stg_ref, htm, hbf, ea, eg, acc, cara, carg, *, bt, t):
    i, j = pl.program_id(0), pl.program_id(1)
    rows = bt * t
    halo = (CONV_WIDTH - 1) * bt

    @pl.when(j == 0)
    def _():
        for b in range(bt):
            hb = _rms(x_ref[b], npre_ref[...]) * (1.0 + mod_ref[1, b:b + 1, :]) + mod_ref[0, b:b + 1, :]
            _to_time_major(htm, b, hb, t, bt)
        hbf[...] = _slab_cols(htm, 0, D_MODEL).astype(BF16)
        acc[...] = jnp.zeros_like(acc)

    @pl.when(i == 0)
    def _():
        cara[j] = pfa_ref[...]
        carg[j] = pfg_ref[...]

    h = hbf[...]

    def conv(e, w_ref, car, cw_ref, cb_ref):
        e[0:halo, :] = car[j]
        e[halo:halo + rows, :] = _dot(h, w_ref[...])
        out = cb_ref[...] + cw_ref[0:1, :] * e[0:rows, :]
        for k in range(1, CONV_WIDTH):
            out = out + cw_ref[k:k + 1, :] * e[k * bt:k * bt + rows, :]
        car[j] = e[rows:rows + halo, :]
        return out

    a = conv(ea, wa_ref, cara, cwa_ref, cba_ref)
    g = conv(eg, wg_ref, carg, cwg_ref, cbg_ref)
    acc[...] += _dot((_silu(g) * a).astype(BF16), wd_ref[...])
    sta_ref[j] = cara[j]
    stg_ref[j] = carg[j]

    @pl.when(j == pl.num_programs(1) - 1)
    def _():
        out = _rms(acc[...], npost_ref[...])
        for c in range(htm.shape[0]):
            htm[c] = out[:, c * LANES:(c + 1) * LANES]
        for b in range(bt):
            y_ref[b] = x_ref[b] + mod_ref[2, b:b + 1, :] * _from_time_major(htm, b, t, bt)


def _ffn_layer(x, mod, npre, npost, wa, wg, cwa, cwg, cba, cbg, wd, pfa, pfg, t, tn):
    bsz, seq, d = x.shape
    t = min(t, seq)
    rows = bsz * t
    halo = (CONV_WIDTH - 1) * bsz
    nj = D_FF // tn
    kern = functools.partial(_ffn_kernel, bt=bsz, t=t)
    full = lambda shape: pl.BlockSpec(shape, lambda i, j: (0,) * len(shape))
    col = lambda r: pl.BlockSpec((r, tn), lambda i, j: (0, j))
    return pl.pallas_call(
        kern,
        grid=(seq // t, nj),
        in_specs=[pl.BlockSpec((bsz, t, d), lambda i, j: (0, i, 0)),
                  full((3, bsz, d)), full((1, d)), full((1, d)),
                  col(d), col(d), col(CONV_WIDTH), col(CONV_WIDTH), col(1), col(1),
                  pl.BlockSpec((tn, d), lambda i, j: (j, 0)), col(halo), col(halo)],
        out_specs=(pl.BlockSpec((bsz, t, d), lambda i, j: (0, i, 0)), full((nj, halo, tn)), full((nj, halo, tn))),
        out_shape=(jax.ShapeDtypeStruct((bsz, seq, d), F32), jax.ShapeDtypeStruct((nj, halo, tn), F32),
                   jax.ShapeDtypeStruct((nj, halo, tn), F32)),
        scratch_shapes=[pltpu.VMEM((d // LANES, rows, LANES), F32), pltpu.VMEM((rows, d), BF16),
                        pltpu.VMEM((rows + halo, tn), F32),
                        pltpu.VMEM((rows + halo, tn), F32), pltpu.VMEM((rows, d), F32),
                        pltpu.VMEM((nj, halo, tn), F32), pltpu.VMEM((nj, halo, tn), F32)],
        compiler_params=_cparams("arbitrary", "arbitrary"),
        name="conv_ffn",
    )(x, mod, npre, npost, wa, wg, cwa, cwg, cba, cbg, wd, pfa, pfg)


def _rope_tables(pos):
    half = HEAD_DIM // 2
    freqs = ROPE_THETA ** (-np.arange(half, dtype=np.float64) / half)
    ang = pos.astype(np.float64)[:, None] * freqs[None, :]
    cos = np.concatenate([np.cos(ang), np.cos(ang)], axis=1)
    sin = np.concatenate([-np.sin(ang), np.sin(ang)], axis=1)
    rep = LANES // HEAD_DIM
    return (jnp.asarray(np.tile(cos, (1, rep)), F32), jnp.asarray(np.tile(sin, (1, rep)), F32))


def _rope(x, cos, sin):
    half = HEAD_DIM // 2
    first = (lax.broadcasted_iota(jnp.int32, cos.shape, 1) % HEAD_DIM) < half
    out = []
    for c in range(x.shape[1] // LANES):
        s = x[:, c * LANES:(c + 1) * LANES]
        swapped = jnp.where(first, pltpu.roll(s, LANES - half, 1), pltpu.roll(s, half, 1))
        out.append(s * cos + swapped * sin)
    return jnp.concatenate(out, axis=1)


QG_PAD = N_HEADS * HEAD_DIM + LANES
NSA_TQ = 256
NSA_TK = 256
NSA_ROWS = 128
NSA_COLS = 256


def _proj_kernel(x_ref, mkv_ref, mq_ref, nkv_ref, nq_ref, wkv_ref, wqg_ref, cos_ref, sin_ref,
                 rows_ref, win_ref, kvb_ref, qp_ref, qr_ref, gate_ref, *rest, bt, t, transposed):
    if transposed:
        vt_ref, hkv, hq = rest
    else:
        hkv, hq = rest
    for b in range(bt):
        xb = x_ref[b]
        xn = xb * lax.rsqrt(jnp.mean(xb * xb, axis=-1, keepdims=True) + EPS)
        hkv[b * t:(b + 1) * t, :] = (xn * nkv_ref[...]) * (1.0 + mkv_ref[1, b]) + mkv_ref[0, b]
        hq[b * t:(b + 1) * t, :] = (xn * nq_ref[...]) * (1.0 + mq_ref[1, b]) + mq_ref[0, b]
    cos, sin = cos_ref[...], sin_ref[...]
    kv = _dot(hkv[...].astype(BF16), wkv_ref[...])
    w = KV_WIDTH
    ksel = _rope(kv[:, 2 * w:3 * w], cos, sin)
    kwin = _rope(kv[:, 4 * w:5 * w], cos, sin)
    rows = jnp.concatenate([kv[:, :2 * w], ksel, kv[:, 3 * w:4 * w]], axis=1)
    win = jnp.concatenate([kwin, kv[:, 5 * w:]], axis=1)
    qg = _dot(hq[...].astype(BF16), wqg_ref[...])
    q = qg[:, :N_HEADS * HEAD_DIM] * (HEAD_DIM ** -0.5)
    qrot = _rope(q, cos, sin)
    gates = _sigmoid(qg[:, N_HEADS * HEAD_DIM:])
    if not transposed:
        for b in range(bt):
            rs = slice(b * t, (b + 1) * t)
            rows_ref[b] = rows[rs]
            win_ref[b] = win[rs]
            kvb_ref[b] = jnp.concatenate([rows[rs, 2 * w:], win[rs]], axis=1).astype(BF16)
            qp_ref[b] = q[rs].astype(BF16)
            qr_ref[b] = qrot[rs].astype(BF16)
            gate_ref[b] = gates[rs]
        return
    rows_ref[0] = rows
    win_ref[0] = win
    kvb_ref[0] = jnp.concatenate([ksel, kwin], axis=1).astype(BF16)
    qp_ref[0] = (q * LOG2E).T.astype(BF16)
    qr_ref[0] = (qrot * LOG2E).T.astype(BF16)
    gate_ref[0] = gates.T
    ones = jnp.ones((HEAD_DIM, NSA_TK), F32)
    for j in range(t // NSA_TK):
        rs = slice(j * NSA_TK, (j + 1) * NSA_TK)
        vt = jnp.concatenate([kv[rs, 3 * w:4 * w], kv[rs, 5 * w:]], axis=1).T
        parts = []
        for h in range(2 * N_KV_HEADS):
            parts += [vt[h * HEAD_DIM:(h + 1) * HEAD_DIM], ones]
        vt_ref[0, j] = jnp.concatenate(parts, axis=0).astype(BF16)


def _proj(x, mkv, mq, nkv, nq, wkv, wqg, cos, sin, bt, t, transposed):
    bsz, seq, d = x.shape
    t = min(t, seq)
    r = bt * t
    kern = functools.partial(_proj_kernel, bt=bt, t=t, transposed=transposed)
    full = lambda shape: pl.BlockSpec(shape, lambda b, i: (0,) * len(shape))
    tile = lambda n: pl.BlockSpec((bt, t, n), lambda b, i: (b, i, 0))
    tile_t = lambda n: pl.BlockSpec((bt, n, t), lambda b, i: (b, 0, i))
    mod = lambda n: pl.BlockSpec((n, bt, 1, d), lambda b, i: (0, b, 0, 0))
    tab = pl.BlockSpec((r, LANES), lambda b, i: (i, 0))
    nq_cols = N_HEADS * HEAD_DIM
    sds = jax.ShapeDtypeStruct
    out_specs = [tile(4 * KV_WIDTH), tile(2 * KV_WIDTH)]
    out_shape = [sds((bsz, seq, 4 * KV_WIDTH), F32), sds((bsz, seq, 2 * KV_WIDTH), F32)]
    if transposed:
        assert bt == 1 and t % NSA_TK == 0
        vrows = 2 * N_KV_HEADS * LANES
        out_specs += [tile(2 * KV_WIDTH), tile_t(nq_cols), tile_t(nq_cols), tile_t(LANES),
                      pl.BlockSpec((1, t // NSA_TK, vrows, NSA_TK), lambda b, i: (b, i, 0, 0))]
        out_shape += [sds((bsz, seq, 2 * KV_WIDTH), BF16), sds((bsz, nq_cols, seq), BF16),
                      sds((bsz, nq_cols, seq), BF16), sds((bsz, LANES, seq), F32),
                      sds((bsz, seq // NSA_TK, vrows, NSA_TK), BF16)]
    else:
        out_specs += [tile(4 * KV_WIDTH), tile(nq_cols), tile(nq_cols), tile(LANES)]
        out_shape += [sds((bsz, seq, 4 * KV_WIDTH), BF16), sds((bsz, seq, nq_cols), BF16),
                      sds((bsz, seq, nq_cols), BF16), sds((bsz, seq, LANES), F32)]
    return pl.pallas_call(
        kern,
        grid=(bsz // bt, seq // t),
        in_specs=[tile(d), mod(2), mod(3), full((1, d)), full((1, d)), full(wkv.shape), full(wqg.shape),
                  tab, tab],
        out_specs=tuple(out_specs),
        out_shape=tuple(out_shape),
        scratch_shapes=[pltpu.VMEM((r, d), F32), pltpu.VMEM((r, d), F32)],
        compiler_params=_cparams("parallel", "parallel"),
        name="kv_q_proj",
    )(x, mkv, mq, nkv, nq, wkv, wqg, cos, sin)


PAGE_SIZE = 128
CHUNKS_PER_PAGE = PAGE_SIZE // CMP_STRIDE
HEADS_PER_SLAB = LANES // HEAD_DIM
CMP_SLABS = 2 * KV_WIDTH // LANES


def _compress_kernel(tbl_ref, *refs, n_pages, transposed):
    page_refs = refs[:n_pages]
    pe_ref, w1_ref, w1bd_ref, b1_ref, w2_ref, perm_ref, out_ref, outt_ref, zs, xs = refs[n_pages:]
    m = n_pages * CHUNKS_PER_PAGE
    for k, pr in enumerate(page_refs):
        zs[k * LANES:(k + 1) * LANES, :] = (pr[0] if transposed else pr[0].T).astype(BF16)
    zp = _dot(zs[...], perm_ref[...])
    for k in range(n_pages):
        xs[k] = zp[k * LANES:(k + 1) * LANES, :].T
    flat = jnp.concatenate([xs[:, l * CHUNKS_PER_PAGE:(l + 1) * CHUNKS_PER_PAGE, :].reshape(m, LANES)
                            for l in range(CMP_STRIDE)], axis=1)
    pb = jnp.dot(pe_ref[0], w1_ref[0], preferred_element_type=F32, precision=lax.Precision.HIGHEST)
    bias = pb[0:1, :CMP_HIDDEN] + pb[1:2, CMP_HIDDEN:] + b1_ref[0]
    pab2 = _dot(flat.astype(BF16), w1bd_ref[0])
    hid = []
    for h in range(HEADS_PER_SLAB):
        pab = pab2[:, h * 2 * CMP_HIDDEN:(h + 1) * 2 * CMP_HIDDEN]
        nxt = pltpu.roll(pab[:, CMP_HIDDEN:], m - 1, 0)
        hid.append(_gelu(pab[:, :CMP_HIDDEN] + nxt + bias))
    out = _dot(jnp.concatenate(hid, axis=1).astype(BF16), w2_ref[0])
    out_ref[0] = out.astype(out_ref.dtype)
    outt_ref[0] = out.T.astype(outt_ref.dtype)


def _compress(src, table, n_pages, pe2, w1f, w1bd, b1, w2bd, transposed):
    bsz = table.shape[0] // n_pages
    m = n_pages * CHUNKS_PER_PAGE
    kern = functools.partial(_compress_kernel, n_pages=n_pages, transposed=transposed)
    slot = lambda s: s // (CMP_SLABS // 2)
    if transposed:
        page = lambda k: pl.BlockSpec((1, LANES, PAGE_SIZE), lambda b, s, tbl: (tbl[b * n_pages + k], s, 0))
    else:
        page = lambda k: pl.BlockSpec((1, PAGE_SIZE, LANES), lambda b, s, tbl: (tbl[b * n_pages + k], 0, s))
    wspec = lambda shape: pl.BlockSpec((1,) + shape, lambda b, s, tbl: (slot(s), 0, 0))
    r = np.arange(PAGE_SIZE)
    perm = jnp.asarray((r % CMP_STRIDE * CHUNKS_PER_PAGE + r // CMP_STRIDE)[:, None] == r[None, :], BF16)
    return pl.pallas_call(
        kern,
        grid_spec=pltpu.PrefetchScalarGridSpec(
            num_scalar_prefetch=1,
            grid=(bsz, CMP_SLABS),
            in_specs=[page(k) for k in range(n_pages)] + [
                wspec(pe2.shape[1:]), wspec(w1f.shape[1:]), wspec(w1bd.shape[1:]), wspec(b1.shape[1:]),
                wspec(w2bd.shape[1:]), pl.BlockSpec(perm.shape, lambda b, s, tbl: (0, 0))],
            out_specs=(pl.BlockSpec((1, m, LANES), lambda b, s, tbl: (b, 0, s)),
                       pl.BlockSpec((1, LANES, m), lambda b, s, tbl: (b, s, 0))),
            scratch_shapes=[pltpu.VMEM((n_pages * LANES, PAGE_SIZE), BF16),
                            pltpu.VMEM((n_pages, PAGE_SIZE, LANES), F32)],
        ),
        out_shape=(jax.ShapeDtypeStruct((bsz, m, 2 * KV_WIDTH), BF16),
                   jax.ShapeDtypeStruct((bsz, 2 * KV_WIDTH, m), BF16)),
        compiler_params=_cparams("parallel", "parallel"),
        name="compress_%d" % n_pages,
    )(table, *([src] * n_pages), pe2, w1f, w1bd, b1, w2bd, perm)


def _sel_matrix(nc, ns, rows, cols):
    r, q = SEL_BLOCK // CMP_STRIDE, CMP_BLOCK // CMP_STRIDE
    offs = (np.arange(r)[:, None] - np.arange(q)[None, :]).reshape(-1)
    target = r * np.arange(ns)[None, :, None] + offs[None, None, :]
    m = (np.arange(nc)[:, None, None] == target).sum(-1).astype(np.float32)
    out = np.zeros((rows, cols), np.float32)
    out[:nc, :ns] = m
    return out


def _select_mask_t(score_t, qpos, ns, n_sel):
    j = lax.broadcasted_iota(jnp.int32, score_t.shape, 0)
    cur = qpos // SEL_BLOCK
    forced = (j == 0) | (j == cur) | (j == cur - 1)
    valid = j * SEL_BLOCK <= qpos
    s = jnp.where(forced, jnp.inf, jnp.where(valid, score_t, -jnp.inf))
    s = jnp.where(j < ns, s, -jnp.inf)
    rank = jnp.zeros(score_t.shape, jnp.int32)
    for jp in range(ns):
        sp = s[jp:jp + 1, :]
        ahead = (sp > s) | ((sp == s) & (jp < j))
        rank = rank + ahead.astype(jnp.int32)
    return ((rank < n_sel) & (j < ns)).astype(F32)


def _nsa_prompt_kernel(x_ref, qpt_ref, qrt_ref, gatet_ref, cmp_ref, cmpt_ref, kk_ref, vt_ref, mod_ref, npost_ref,
                       wot_ref, mselt_ref, expandt_ref, y_ref, bias_sel, bias_win, q4t_s, m_s, a_s, acc_s, s_s, p_s,
                       ot_s, *, tq, tk, seq, ns):
    qt = pl.program_id(1)
    t0 = qt * tq
    w = KV_WIDTH
    grp = GQA_GROUP
    nq = grp * tq
    nc_pad = cmp_ref.shape[1]
    qpos_row = t0 + lax.broadcasted_iota(jnp.int32, (1, tq), 1)
    cends = lax.broadcasted_iota(jnp.int32, (nc_pad, 1), 0) * CMP_STRIDE + (CMP_BLOCK - 1)
    cvalid = cends <= qpos_row
    cbias = jnp.concatenate([jnp.where(cvalid, 0.0, NEG)] * grp, axis=1)
    cmask = jnp.concatenate([cvalid.astype(F32)] * grp, axis=1)
    n_sel_tiles = (t0 + tq + tk - 1) // tk
    n_win_tiles = (WINDOW + tq + tk - 1) // tk

    for i in range(n_win_tiles):
        kpos = t0 + tq - (i + 1) * tk + lax.broadcasted_iota(jnp.int32, (tk, 1), 0)
        diff = qpos_row - kpos
        bias_win[i] = jnp.where((diff >= 0) & (diff < WINDOW) & (kpos >= 0), 0.0, NEG)
    kpos_all = lax.broadcasted_iota(jnp.int32, (seq, 1), 0)

    def attend(kcol, vrow, n_tiles, unroll, rem_step, start_of, bias_of):
        m_s[...] = jnp.full(m_s.shape, NEG, F32)
        acc_s[...] = jnp.zeros(acc_s.shape, F32)

        def tile(i, u):
            k0 = pl.multiple_of(start_of(i), tk)
            k = kk_ref[0, pl.ds(k0, tk), kcol:kcol + HEAD_DIM]
            vt = vt_ref[0, k0 // tk, vrow:vrow + LANES, :]
            s_s[u] = _dot(k, q4t_s[...])
            for c in range(nq // NSA_COLS):
                cols = slice(c * NSA_COLS, (c + 1) * NSA_COLS)
                top = None
                for r in range(tk // NSA_ROWS):
                    rows = slice(r * NSA_ROWS, (r + 1) * NSA_ROWS)
                    s = s_s[u, rows, cols] + bias_of(i, k0, r * NSA_ROWS, (c * NSA_COLS) % tq)
                    s_s[u, rows, cols] = s
                    cur = jnp.max(s, axis=0, keepdims=True)
                    top = cur if top is None else jnp.maximum(top, cur)
                m_old = m_s[:, cols]
                m_new = jnp.maximum(m_old, top)
                m_rep = jnp.concatenate([m_new] * (NSA_ROWS // SUBLANES), axis=0)
                for r in range(tk // NSA_ROWS):
                    rows = slice(r * NSA_ROWS, (r + 1) * NSA_ROWS)
                    p_s[u, rows, cols] = jnp.exp2(s_s[u, rows, cols] - m_rep).astype(BF16)
                a_s[u, :, cols] = jnp.exp2(m_old - m_new)
                m_s[:, cols] = m_new
            alpha = jnp.concatenate([a_s[u]] * (LANES // SUBLANES), axis=0)
            acc_s[...] = alpha * acc_s[...] + _dot(vt, p_s[u])

        def body(j, carry):
            for u in range(unroll):
                tile(j * unroll + u, u)
            return carry

        n_full = n_tiles // unroll
        lax.fori_loop(0, n_full, body, 0)
        if rem_step:
            @pl.when(n_tiles - n_full * unroll >= rem_step)
            def _():
                for u in range(rem_step):
                    tile(n_full * unroll + u, u)
        return acc_s[:HEAD_DIM, :] / acc_s[HEAD_DIM:, :]

    for kh in range(N_KV_HEADS):
        heads = [kh * grp + g for g in range(grp)]
        for g, h in enumerate(heads):
            q4t_s[:, g * tq:(g + 1) * tq] = qpt_ref[0, h * HEAD_DIM:(h + 1) * HEAD_DIM, :]
        s = _dot(cmp_ref[0, :, kh * HEAD_DIM:(kh + 1) * HEAD_DIM], q4t_s[...]) + cbias
        e = jnp.exp2(s - jnp.max(s, axis=0, keepdims=True))
        p = e / jnp.sum(e, axis=0, keepdims=True) * cmask
        o_cmp = _dot(cmpt_ref[0, w + kh * HEAD_DIM:w + (kh + 1) * HEAD_DIM, :], p.astype(BF16))
        psum = p[:, 0:tq]
        for g in range(1, grp):
            psum = psum + p[:, g * tq:(g + 1) * tq]
        score_t = jnp.dot(mselt_ref[...], psum, preferred_element_type=F32, precision=lax.Precision.HIGHEST)
        sel_t = _select_mask_t(score_t, qpos_row, ns, N_SEL)
        picked = _dot(expandt_ref[...], sel_t.astype(BF16))
        bias_sel[...] = jnp.where((picked > 0.5) & (kpos_all <= qpos_row), 0.0, NEG)

        for g, h in enumerate(heads):
            q4t_s[:, g * tq:(g + 1) * tq] = qrt_ref[0, h * HEAD_DIM:(h + 1) * HEAD_DIM, :]
        o_sel = attend(kh * HEAD_DIM, kh * LANES, n_sel_tiles, 2 * (tq // tk), tq // tk, lambda i: i * tk,
                       lambda i, k0, r0, c0: bias_sel[pl.ds(k0 + r0, NSA_ROWS), c0:c0 + NSA_COLS])
        o_win = attend(w + kh * HEAD_DIM, (N_KV_HEADS + kh) * LANES, n_win_tiles, n_win_tiles, 0,
                       lambda i: jnp.maximum(t0 + tq - (i + 1) * tk, 0),
                       lambda i, k0, r0, c0: bias_win[i, r0:r0 + NSA_ROWS, c0:c0 + NSA_COLS])
        for g, h in enumerate(heads):
            cs = slice(g * tq, (g + 1) * tq)
            gc = gatet_ref[0, h:h + 1, :]
            gs = gatet_ref[0, N_HEADS + h:N_HEADS + h + 1, :]
            gw = gatet_ref[0, 2 * N_HEADS + h:2 * N_HEADS + h + 1, :]
            ot_s[h * HEAD_DIM:(h + 1) * HEAD_DIM, :] = (
                gc * o_cmp[:, cs] + gs * o_sel[:, cs] + gw * o_win[:, cs]).astype(BF16)

    out = _dot(wot_ref[...], ot_s[...]).T
    y_ref[0] = x_ref[0] + mod_ref[0] * _rms(out, npost_ref[...])


def _nsa_prompt(x, qpt, qrt, gatet, cmp, cmpt, kk, vt, gate_mod, npost, wot, tq, tk):
    bsz, seq, d = x.shape
    tq = min(tq, seq)
    assert tq % tk == 0 and seq % tq == 0 and tq % NSA_COLS == 0 and vt.shape[3] == tk
    nc = (seq - CMP_BLOCK) // CMP_STRIDE + 1
    ns = -(-seq // SEL_BLOCK)
    nc_pad = cmp.shape[1]
    j_pad = -(-ns // 16) * 16
    n_win = (WINDOW + tq + tk - 1) // tk
    mselt = jnp.asarray(_sel_matrix(nc, ns, nc_pad, j_pad).T)
    expandt = jnp.asarray((np.arange(seq)[:, None] // SEL_BLOCK == np.arange(j_pad)[None, :]), BF16)
    kern = functools.partial(_nsa_prompt_kernel, tq=tq, tk=tk, seq=seq, ns=ns)
    full = lambda shape: pl.BlockSpec(shape, lambda b, i: (0,) * len(shape))
    tile = lambda n: pl.BlockSpec((1, tq, n), lambda b, i: (b, i, 0))
    tile_t = lambda n: pl.BlockSpec((1, n, tq), lambda b, i: (b, 0, i))
    per_b = lambda *shape: pl.BlockSpec((1,) + shape, lambda b, i: (b,) + (0,) * len(shape))
    nq = GQA_GROUP * tq
    return pl.pallas_call(
        kern,
        grid=(bsz, seq // tq),
        in_specs=[tile(d), tile_t(qpt.shape[1]), tile_t(qrt.shape[1]), tile_t(LANES), per_b(*cmp.shape[1:]),
                  per_b(*cmpt.shape[1:]), per_b(*kk.shape[1:]), per_b(*vt.shape[1:]), per_b(1, d), full((1, d)),
                  full(wot.shape), full(mselt.shape), full(expandt.shape)],
        out_specs=tile(d),
        out_shape=jax.ShapeDtypeStruct((bsz, seq, d), F32),
        scratch_shapes=[pltpu.VMEM((seq, tq), F32), pltpu.VMEM((n_win, tk, tq), F32),
                        pltpu.VMEM((HEAD_DIM, nq), BF16), pltpu.VMEM((SUBLANES, nq), F32),
                        pltpu.VMEM((n_win, SUBLANES, nq), F32), pltpu.VMEM((LANES, nq), F32),
                        pltpu.VMEM((n_win, tk, nq), F32), pltpu.VMEM((n_win, tk, nq), BF16),
                        pltpu.VMEM((d, tq), BF16)],
        compiler_params=_cparams("parallel", "arbitrary"),
        name="nsa_prompt",
    )(x, qpt, qrt, gatet, cmp, cmpt, kk, vt, gate_mod, npost, wot, mselt, expandt)


def _nsa_sample_kernel(tbl_ref, *refs, n_pages, pages_per_step, past_len, n_steps, ns):
    page_refs = refs[:pages_per_step]
    (qp_ref, qr_ref, gate_ref, cmp_ref, new_ref, wc_ref, mselt_ref, expand_ref, expand_new_ref, out_ref,
     selt, m_s, l_s, acc_s, ocmp_s) = refs[pages_per_step:]
    pc = pl.program_id(1)
    w = KV_WIDTH
    nrow = qp_ref.shape[1]
    step_of_row = lax.broadcasted_iota(jnp.int32, (nrow, 1), 0) % n_steps
    qpos_col = past_len + step_of_row
    qr = qr_ref[0]

    @pl.when(pc == 0)
    def _():
        nc_pad = cmp_ref.shape[1]
        cends = lax.broadcasted_iota(jnp.int32, (1, nc_pad), 1) * CMP_STRIDE + (CMP_BLOCK - 1)
        cvalid = cends <= qpos_col
        s = jnp.where(cvalid, _dot_nt(qp_ref[0], cmp_ref[0, :, :w]), NEG)
        e = jnp.exp(s - jnp.max(s, axis=-1, keepdims=True))
        p = jnp.where(cvalid, e / jnp.sum(e, axis=-1, keepdims=True), 0.0)
        ocmp_s[...] = _dot(p.astype(BF16), cmp_ref[0, :, w:])
        grp = nrow // GQA_GROUP
        psum = p[0:grp]
        for g in range(1, GQA_GROUP):
            psum = psum + p[g * grp:(g + 1) * grp]
        psum = jnp.concatenate([psum] * GQA_GROUP, axis=0)
        score_t = lax.dot_general(mselt_ref[...], psum, (((1,), (1,)), ((), ())),
                                  preferred_element_type=F32, precision=lax.Precision.HIGHEST)
        qpos_row = past_len + lax.broadcasted_iota(jnp.int32, (1, nrow), 1) % n_steps
        selt[...] = _select_mask_t(score_t, qpos_row, ns, N_SEL).astype(BF16)
        m_s[...] = jnp.full(m_s.shape, NEG, F32)
        l_s[...] = jnp.zeros(l_s.shape, F32)
        acc_s[...] = jnp.zeros(acc_s.shape, F32)

    def flash(s, allowed, pv):
        s = jnp.where(allowed, s, NEG)
        m_old = m_s[...]
        m_new = jnp.maximum(m_old, jnp.max(s, axis=-1, keepdims=True))
        alpha = jnp.exp(m_old - m_new)
        p = jnp.exp(s - m_new)
        l_s[...] = alpha * l_s[...] + jnp.sum(p, axis=-1, keepdims=True)
        acc_s[...] = alpha * acc_s[...] + pv(p.astype(BF16))
        m_s[...] = m_new

    kt = jnp.concatenate([pr[0, :w, :] for pr in page_refs], axis=1).astype(BF16)
    vt = jnp.concatenate([pr[0, w:, :] for pr in page_refs], axis=1).astype(BF16)
    allowed = lax.dot_general(selt[...], expand_ref[...], (((0,), (0,)), ((), ())),
                              preferred_element_type=F32) > 0.5
    flash(_dot(qr, kt), allowed, lambda p: _dot_nt(p, vt))

    @pl.when(pc == pl.num_programs(1) - 1)
    def _():
        pad = jnp.zeros((LANES - n_steps, 4 * w), F32)
        new = jnp.concatenate([new_ref[0].astype(F32), pad], axis=0).astype(BF16)
        key_step = lax.broadcasted_iota(jnp.int32, (1, LANES), 1)
        causal_new = key_step <= step_of_row
        picked = lax.dot_general(selt[...], expand_new_ref[...], (((0,), (0,)), ((), ())),
                                 preferred_element_type=F32) > 0.5
        flash(_dot_nt(qr, new[:, :w]), causal_new & picked, lambda p: _dot(p, new[:, w:2 * w]))
        o_sel = acc_s[...] / l_s[...]

        w_buf = wc_ref.shape[2]
        buf_idx = lax.broadcasted_iota(jnp.int32, (1, w_buf), 1)
        diff_buf = step_of_row + w_buf - buf_idx
        s_buf = jnp.where(diff_buf < WINDOW, _dot(qr, wc_ref[0, :w, :].astype(BF16)), NEG)
        s_new = jnp.where(causal_new, _dot_nt(qr, new[:, 2 * w:3 * w]), NEG)
        sw = jnp.concatenate([s_buf, s_new], axis=1)
        ew = jnp.exp(sw - jnp.max(sw, axis=-1, keepdims=True))
        pw = (ew / jnp.sum(ew, axis=-1, keepdims=True)).astype(BF16)
        o_win = _dot_nt(pw[:, :w_buf], wc_ref[0, w:, :].astype(BF16)) + _dot(pw[:, w_buf:], new[:, 3 * w:])
        g = gate_ref[0]
        out_ref[0] = g[:, 0:1] * ocmp_s[...] + g[:, 1:2] * o_sel + g[:, 2:3] * o_win


SAMPLE_PAGES_PER_STEP = 16


def _nsa_sample(pool, table, n_pages, qp_bd, qr_bd, gates_r, cmp, new_kvb, wcache, past_len):
    bsz, nrow, _ = qp_bd.shape
    n_steps = new_kvb.shape[1]
    pps = min(SAMPLE_PAGES_PER_STEP, n_pages)
    nc = (past_len + n_steps - CMP_BLOCK) // CMP_STRIDE + 1
    ns = -(-(past_len + n_steps) // SEL_BLOCK)
    assert ns == past_len // SEL_BLOCK + 1 and n_steps <= SEL_BLOCK
    nc_pad = cmp.shape[1]
    j_pad = -(-ns // 16) * 16
    mselt = jnp.asarray(_sel_matrix(nc, ns, nc_pad, j_pad).T)
    expand = jnp.asarray((np.arange(j_pad)[:, None] == np.arange(past_len)[None, :] // SEL_BLOCK), BF16)
    expand_new = jnp.asarray(np.broadcast_to(np.arange(j_pad)[:, None] == ns - 1, (j_pad, LANES)), BF16)
    kern = functools.partial(_nsa_sample_kernel, n_pages=n_pages, pages_per_step=pps, past_len=past_len,
                             n_steps=n_steps, ns=ns)
    sel_rows = 1
    page = lambda k: pl.BlockSpec((1, 2 * KV_WIDTH, PAGE_SIZE),
                                  lambda b, c, tbl: (tbl[b * n_pages + c * pps + k], sel_rows, 0))
    per_b = lambda r, n: pl.BlockSpec((1, r, n), lambda b, c, tbl: (b, 0, 0))
    full = lambda shape: pl.BlockSpec(shape, lambda b, c, tbl: (0,) * len(shape))
    return pl.pallas_call(
        kern,
        grid_spec=pltpu.PrefetchScalarGridSpec(
            num_scalar_prefetch=1,
            grid=(bsz, n_pages // pps),
            in_specs=[page(k) for k in range(pps)] + [
                per_b(nrow, KV_WIDTH), per_b(nrow, KV_WIDTH), per_b(nrow, LANES), per_b(nc_pad, 2 * KV_WIDTH),
                per_b(n_steps, 4 * KV_WIDTH), per_b(2 * KV_WIDTH, wcache.shape[2]), full(mselt.shape),
                pl.BlockSpec((j_pad, pps * PAGE_SIZE), lambda b, c, tbl: (0, c)), full(expand_new.shape)],
            out_specs=per_b(nrow, KV_WIDTH),
            scratch_shapes=[pltpu.VMEM((j_pad, nrow), BF16), pltpu.VMEM((nrow, 1), F32),
                            pltpu.VMEM((nrow, 1), F32), pltpu.VMEM((nrow, KV_WIDTH), F32),
                            pltpu.VMEM((nrow, KV_WIDTH), F32)],
        ),
        out_shape=jax.ShapeDtypeStruct((bsz, nrow, KV_WIDTH), F32),
        compiler_params=_cparams("parallel", "arbitrary"),
        name="nsa_sample",
    )(table, *([pool] * pps), qp_bd, qr_bd, gates_r, cmp, new_kvb, wcache, mselt, expand, expand_new)


def _oproj_kernel(o_ref, x_ref, gate_ref, npost_ref, wo_ref, y_ref):
    out = _dot(o_ref[...].astype(BF16), wo_ref[...])
    y_ref[...] = x_ref[...] + gate_ref[...] * _rms(out, npost_ref[...])


def _oproj(o, x, gate, npost, wo):
    return pl.pallas_call(
        _oproj_kernel,
        out_shape=jax.ShapeDtypeStruct(x.shape, F32),
        compiler_params=pltpu.CompilerParams(vmem_limit_bytes=VMEM_LIMIT),
        name="nsa_out_proj",
    )(o, x, gate, npost, wo)


S5_T = 64
FFN_T = 64
FFN_TN = D_FF // 2


def _prepare(p):
    w = {}
    a_re, a_im, zb_re, zb_im = _s5_prep(p['ssm_lam_re'][0], p['ssm_lam_im'][0], p['ssm_log_dt'][0],
                                        p['ssm_b_re'][0], p['ssm_b_im'][0])
    w['s5_a'] = jnp.stack([a_re.reshape(-1), a_im.reshape(-1)])
    w['s5_wb'] = jnp.concatenate([_s5_block_diag_in(zb_re), _s5_block_diag_in(zb_im)], axis=-1).astype(BF16)
    w['s5_wcre'] = _s5_block_diag_out(p['ssm_c_re'][0]).astype(BF16)
    w['s5_wcim'] = _s5_block_diag_out(p['ssm_c_im'][0]).astype(BF16)
    w['w_glu'] = p['w_glu'][0].astype(BF16)
    w['ffn_wa'] = p['ffn_w_up'][:, :, :D_FF].astype(BF16)
    w['ffn_wg'] = p['ffn_w_up'][:, :, D_FF:].astype(BF16)
    w['ffn_wd'] = p['ffn_w_down'].astype(BF16)
    w['w_kv'] = p['w_kv'].astype(BF16)
    w['w_qg'] = jnp.pad(p['w_qg'][0], ((0, 0), (0, QG_PAD - p['w_qg'].shape[2]))).astype(BF16)
    w['w_o'] = p['w_o'][0].astype(BF16)
    w['w_o_t'] = p['w_o'][0].T.astype(BF16)
    half = CMP_BLOCK // 2
    w1r = p['cmp_w1'].reshape(2, CMP_BLOCK, HEAD_DIM, CMP_HIDDEN)
    w['cmp_w1'] = jnp.concatenate([w1r[:, :half], w1r[:, half:]], axis=-1).reshape(
        2, half * HEAD_DIM, 2 * CMP_HIDDEN)
    w['cmp_pe'] = p['cmp_pe'].reshape(2, 2, half * HEAD_DIM)
    w['cmp_b1'] = p['cmp_b1'].reshape(2, 1, CMP_HIDDEN)
    eye = jnp.eye(HEADS_PER_SLAB, dtype=F32)
    w['cmp_w1bd'] = jnp.einsum('sldn,ab->sladbn', w['cmp_w1'].reshape(2, half, HEAD_DIM, 2 * CMP_HIDDEN),
                               eye).reshape(2, half * LANES, HEADS_PER_SLAB * 2 * CMP_HIDDEN).astype(BF16)
    w['cmp_w2'] = jnp.einsum('shd,ab->sahbd', p['cmp_w2'], eye).reshape(
        2, HEADS_PER_SLAB * CMP_HIDDEN, LANES).astype(BF16)
    return w


def _split3(m):
    return m.reshape(m.shape[0], 3, D_MODEL).transpose(1, 0, 2)


def _ffn_call(x, mod, layer, p, w, prefix):
    bsz = x.shape[0]
    halo = (CONV_WIDTH - 1) * bsz
    pf = prefix.transpose(1, 0, 2).reshape(halo, 2 * D_FF)
    cw, cb = p['ffn_conv_w'][layer], p['ffn_conv_b'][layer].reshape(1, -1)
    y, sta, stg = _ffn_layer(x, mod, p['norm_pre'][layer, 1:2], p['norm_post'][layer, 1:2],
                             w['ffn_wa'][layer], w['ffn_wg'][layer], cw[:, :D_FF], cw[:, D_FF:],
                             cb[:, :D_FF], cb[:, D_FF:], w['ffn_wd'][layer], pf[:, :D_FF], pf[:, D_FF:],
                             FFN_T, FFN_TN)
    st = jnp.stack([sta, stg]).reshape(2, D_FF // FFN_TN, CONV_WIDTH - 1, bsz, FFN_TN)
    st = st.transpose(3, 2, 0, 1, 4).reshape(bsz, CONV_WIDTH - 1, 2 * D_FF)
    return y, st


def _layer0(x, mods, p, w, h0, prefix):
    bsz = x.shape[0]
    y, s = _s5_layer(x, _split3(mods[0]), p['norm_pre'][0, 0:1], p['norm_post'][0, 0:1], w['s5_wb'],
                     w['s5_wcre'], w['s5_wcim'], w['s5_a'], p['ssm_d'][0:1], w['w_glu'], p['b_glu'][0:1],
                     h0, S5_T)
    s = s.reshape(2, bsz, N_GROUPS, SSM_STATE)
    y, st = _ffn_call(y, _split3(mods[1]), 0, p, w, prefix)
    return y, s[0], s[1], st


PROJ_T = 512


def _proj_call(x, mods, modkv, p, w, pos, bt):
    bsz, seq, _ = x.shape
    cos, sin = _rope_tables(pos)
    if bt > 1:
        cos, sin = jnp.tile(cos, (bt, 1)), jnp.tile(sin, (bt, 1))
    mkv = modkv.reshape(bsz, 2, 1, D_MODEL).transpose(1, 0, 2, 3)
    mq = _split3(mods[2]).reshape(3, bsz, 1, D_MODEL)
    return _proj(x, mkv, mq, p['norm_kv'].reshape(1, D_MODEL), p['norm_pre'][1, 0:1], w['w_kv'], w['w_qg'],
                 cos, sin, bt, PROJ_T, bt == 1)


def _prompt_group(x, mods, modkv, p, w):
    bsz, seq, _ = x.shape
    zeros_state = jnp.zeros((2, bsz, N_STATE), F32)
    zeros_prefix = jnp.zeros((bsz, CONV_WIDTH - 1, 2 * D_FF), F32)
    x1, s_re, s_im, conv0 = _layer0(x, mods, p, w, zeros_state, zeros_prefix)
    rows, win, kk, qpt, qrt, gatet, vt = _proj_call(x1, mods, modkv, p, w, np.arange(seq), 1)
    n_pages = seq // PAGE_SIZE
    table = jnp.arange(bsz * n_pages, dtype=jnp.int32)
    cmp, cmpt = _compress(rows.reshape(bsz * n_pages, PAGE_SIZE, 4 * KV_WIDTH), table, n_pages,
                          w['cmp_pe'], w['cmp_w1'], w['cmp_w1bd'], w['cmp_b1'], w['cmp_w2'], False)
    gate_mod = _split3(mods[2])[2].reshape(bsz, 1, D_MODEL)
    x2 = _nsa_prompt(x1, qpt, qrt, gatet, cmp, cmpt, kk, vt, gate_mod, p['norm_post'][1, 0:1], w['w_o_t'],
                     NSA_TQ, NSA_TK)
    y, conv1 = _ffn_call(x2, _split3(mods[3]), 1, p, w, zeros_prefix)
    kvd = (N_KV_HEADS, HEAD_DIM)
    return (y, rows.reshape(bsz, seq, 4, *kvd), win[:, -min(WINDOW, seq):].reshape(bsz, -1, 2, *kvd),
            s_re[None], s_im[None], jnp.stack([conv0, conv1]))


def _block_diag_rows(q):
    bsz, steps, _ = q.shape
    q5 = q.reshape(bsz, steps, N_KV_HEADS, GQA_GROUP, HEAD_DIM)
    eye = jnp.eye(N_KV_HEADS, dtype=q.dtype)
    return jnp.einsum('bqkgd,kj->bgkqjd', q5, eye).reshape(bsz, GQA_GROUP * N_KV_HEADS * steps, KV_WIDTH)


def _sample_group(x, mods, modkv, p, w, pool, page_table, wcache, ssm_re, ssm_im, conv_state):
    bsz, steps, _ = x.shape
    n_pages = page_table.shape[1]
    past_len = n_pages * PAGE_SIZE
    h0 = jnp.stack([ssm_re[0].reshape(bsz, N_STATE), ssm_im[0].reshape(bsz, N_STATE)])
    x1, s_re, s_im, conv0 = _layer0(x, mods, p, w, h0, conv_state[0])
    rows, win, kvb, qp, qr, gates = _proj_call(x1, mods, modkv, p, w, past_len + np.arange(steps), bsz)
    pool2 = pool.transpose(0, 2, 3, 4, 1).reshape(pool.shape[0], 4 * KV_WIDTH, PAGE_SIZE)
    wcache2 = wcache.transpose(0, 2, 3, 4, 1).reshape(bsz, 2 * KV_WIDTH, wcache.shape[1])
    table = page_table.reshape(-1)
    cmp, _ = _compress(pool2, table, n_pages, w['cmp_pe'], w['cmp_w1'], w['cmp_w1bd'], w['cmp_b1'], w['cmp_w2'],
                       True)
    nrow = GQA_GROUP * N_KV_HEADS * steps
    g5 = gates[:, :, :3 * N_HEADS].reshape(bsz, steps, 3, N_KV_HEADS, GQA_GROUP)
    gates_r = jnp.pad(g5.transpose(0, 4, 3, 1, 2).reshape(bsz, nrow, 3), ((0, 0), (0, 0), (0, LANES - 3)))
    o_bd = _nsa_sample(pool2, table, n_pages, _block_diag_rows(qp), _block_diag_rows(qr), gates_r, cmp, kvb,
                       wcache2, past_len)
    o6 = o_bd.reshape(bsz, GQA_GROUP, N_KV_HEADS, steps, N_KV_HEADS, HEAD_DIM)
    o = jnp.einsum('bgkqjd,kj->bqkgd', o6, jnp.eye(N_KV_HEADS, dtype=F32)).reshape(bsz * steps, D_MODEL)
    gate_mod = jnp.repeat(_split3(mods[2])[2], steps, axis=0)
    x2 = _oproj(o, x1.reshape(bsz * steps, D_MODEL), gate_mod, p['norm_post'][1, 0:1], w['w_o'])
    y, conv1 = _ffn_call(x2.reshape(bsz, steps, D_MODEL), _split3(mods[3]), 1, p, w, conv_state[1])
    kvd = (N_KV_HEADS, HEAD_DIM)
    return (y, rows.reshape(bsz, steps, 4, *kvd), win.reshape(bsz, steps, 2, *kvd),
            s_re[None], s_im[None], jnp.stack([conv0, conv1]))


def kernel(x_prompt, x_sample, cache_nsa_paged, cache_nsa_window, state_ssm_re, state_ssm_im, state_ffn_conv,
           page_table, c_prompt, c_sample, norm_pre, norm_post, w_mod, b_mod, ssm_lam_re, ssm_lam_im,
           ssm_log_dt, ssm_b_re, ssm_b_im, ssm_c_re, ssm_c_im, ssm_d, w_glu, b_glu, ffn_w_up, ffn_conv_w,
           ffn_conv_b, ffn_w_down, norm_kv, w_mod_kv, b_mod_kv, w_kv, cmp_pe, cmp_w1, cmp_b1, cmp_w2,
           w_qg, w_o):
    p = {'norm_pre': norm_pre, 'norm_post': norm_post, 'w_mod': w_mod, 'b_mod': b_mod,
         'ssm_lam_re': ssm_lam_re, 'ssm_lam_im': ssm_lam_im, 'ssm_log_dt': ssm_log_dt,
         'ssm_b_re': ssm_b_re, 'ssm_b_im': ssm_b_im, 'ssm_c_re': ssm_c_re, 'ssm_c_im': ssm_c_im,
         'ssm_d': ssm_d, 'w_glu': w_glu, 'b_glu': b_glu, 'ffn_w_up': ffn_w_up, 'ffn_conv_w': ffn_conv_w,
         'ffn_conv_b': ffn_conv_b, 'ffn_w_down': ffn_w_down, 'norm_kv': norm_kv, 'w_mod_kv': w_mod_kv,
         'b_mod_kv': b_mod_kv, 'w_kv': w_kv, 'cmp_pe': cmp_pe, 'cmp_w1': cmp_w1, 'cmp_b1': cmp_b1,
         'cmp_w2': cmp_w2, 'w_qg': w_qg, 'w_o': w_o}
    w = _prepare(p)
    nb, ns = x_prompt.shape[0], x_sample.shape[0]
    c_all = jnp.concatenate([c_prompt, c_sample])
    mods = _mods(c_all, w_mod.reshape(4, D_MODEL, 3 * D_MODEL), b_mod.reshape(4, 3 * D_MODEL))
    modkv = _mods(c_all, w_mod_kv[None], b_mod_kv[None])[0]
    y_p, rows_p, win_p, sre_p, sim_p, conv_p = _prompt_group(x_prompt, mods[:, :nb], modkv[:nb], p, w)
    y_s, rows_s, win_s, sre_s, sim_s, conv_s = _sample_group(
        x_sample, mods[:, nb:], modkv[nb:], p, w, cache_nsa_paged, page_table, cache_nsa_window,
        state_ssm_re, state_ssm_im, state_ffn_conv)
    return (y_p, y_s, rows_p, rows_s, win_p, win_s, sre_p, sim_p, sre_s, sim_s, conv_p, conv_s)
```

```python
import functools
import math

import numpy as np
import jax
import jax.numpy as jnp
from jax import lax
from jax.experimental import pallas as pl
from jax.experimental.pallas import tpu as pltpu

D_MODEL = 1024
SSM_GROUP = 16
N_GROUPS = D_MODEL // SSM_GROUP
SSM_STATE = 64
N_STATE = N_GROUPS * SSM_STATE
N_HEADS = 16
HEAD_DIM = 64
N_KV_HEADS = 4
GQA_GROUP = N_HEADS // N_KV_HEADS
KV_WIDTH = N_KV_HEADS * HEAD_DIM
CMP_BLOCK = 32
CMP_STRIDE = 16
CMP_HIDDEN = 2 * HEAD_DIM
SEL_BLOCK = 64
N_SEL = 16
WINDOW = 512
ROPE_THETA = 10000.0
D_FF = 2816
CONV_WIDTH = 3
EPS = 1e-6
NEG = -1e30
LOG2E = math.log2(math.e)

SUBLANES = 8
LANES = 128
MXU_DIM = 256
VMEM_LIMIT = 56 * 1024 * 1024

F32 = jnp.float32
BF16 = jnp.bfloat16


def _cparams(*sem):
    return pltpu.CompilerParams(dimension_semantics=sem, vmem_limit_bytes=VMEM_LIMIT)


def _sigmoid(x):
    return 1.0 / (1.0 + jnp.exp(-x))


def _silu(x):
    return x * _sigmoid(x)


def _gelu(x):
    return 0.5 * x * (1.0 + jnp.tanh(math.sqrt(2.0 / math.pi) * (x + 0.044715 * (x * x * x))))


def _rms(x, g):
    return x * lax.rsqrt(jnp.mean(x * x, axis=-1, keepdims=True) + EPS) * g


def _dot(a, b):
    return jnp.dot(a, b, preferred_element_type=F32)


def _dot_nt(a, b):
    return lax.dot_general(a, b, (((1,), (1,)), ((), ())), preferred_element_type=F32)


def _mods_kernel(c_ref, w_ref, b_ref, o_ref):
    c = _silu(c_ref[...])
    o_ref[0] = jnp.dot(c, w_ref[0], preferred_element_type=F32,
                       precision=lax.Precision.HIGHEST) + b_ref[0]


def _mods(c, w, b):
    s, d, n = w.shape
    m = c.shape[0]
    tn = 1024
    return pl.pallas_call(
        _mods_kernel,
        grid=(s, n // tn),
        in_specs=[pl.BlockSpec((m, d), lambda i, j: (0, 0)),
                  pl.BlockSpec((1, d, tn), lambda i, j: (i, 0, j)),
                  pl.BlockSpec((1, 1, tn), lambda i, j: (i, 0, j))],
        out_specs=pl.BlockSpec((1, m, tn), lambda i, j: (i, 0, j)),
        out_shape=jax.ShapeDtypeStruct((s, m, n), F32),
        compiler_params=_cparams("parallel", "parallel"),
        name="adaln_mods",
    )(c, w, b.reshape(s, 1, n))


def _s5_prep_kernel(lr_ref, li_ref, ldt_ref, bre_ref, bim_ref, are_ref, aim_ref, zbre_ref, zbim_ref):
    lr, li = lr_ref[...], li_ref[...]
    dt = jnp.exp(ldt_ref[...])
    mag = jnp.exp(lr * dt)
    a_re, a_im = mag * jnp.cos(li * dt), mag * jnp.sin(li * dt)
    den = lr * lr + li * li
    ir, ii = lr / den, -li / den
    z_re = (a_re - 1.0) * ir - a_im * ii
    z_im = (a_re - 1.0) * ii + a_im * ir
    are_ref[...] = a_re
    aim_ref[...] = a_im
    bre, bim = bre_ref[...], bim_ref[...]
    zr, zi = z_re[:, None, :], z_im[:, None, :]
    zbre_ref[...] = zr * bre - zi * bim
    zbim_ref[...] = zr * bim + zi * bre


def _s5_prep(lam_re, lam_im, log_dt, b_re, b_im):
    g, p, c = b_re.shape
    return pl.pallas_call(
        _s5_prep_kernel,
        out_shape=(jax.ShapeDtypeStruct((g, p), F32), jax.ShapeDtypeStruct((g, p), F32),
                   jax.ShapeDtypeStruct((g, c, p), F32), jax.ShapeDtypeStruct((g, c, p), F32)),
        name="s5_prep",
    )(lam_re, lam_im, log_dt.reshape(g, 1), b_re.transpose(0, 2, 1), b_im.transpose(0, 2, 1))


S5_KBLOCKS = D_MODEL // MXU_DIM
S5_GPB = MXU_DIM // SSM_GROUP
S5_SPB = S5_GPB * SSM_STATE


def _s5_block_diag_in(zb):
    z = zb.reshape(S5_KBLOCKS, S5_GPB, SSM_GROUP, SSM_STATE)
    eye = jnp.eye(S5_GPB, dtype=zb.dtype)
    return jnp.einsum('kgcp,gh->kgchp', z, eye).reshape(S5_KBLOCKS, MXU_DIM, S5_SPB)


def _s5_block_diag_out(cm):
    z = cm.reshape(S5_KBLOCKS, S5_GPB, SSM_GROUP, SSM_STATE)
    eye = jnp.eye(S5_GPB, dtype=cm.dtype)
    return jnp.einsum('kgcp,gh->kgphc', z, eye).reshape(S5_KBLOCKS, S5_SPB, MXU_DIM)


S5_LANE_BLOCKS = 8


def _to_time_major(dst, b, val, t, bt):
    for c in range(val.shape[1] // LANES):
        dst[c, pl.ds(b, t, stride=bt), :] = val[:, c * LANES:(c + 1) * LANES]


def _from_time_major(src, b, t, bt):
    return jnp.concatenate([src[c, pl.ds(b, t, stride=bt), :] for c in range(src.shape[0])], axis=1)


def _slab_cols(src, c0, c1):
    return jnp.concatenate([src[c] for c in range(c0 // LANES, c1 // LANES)], axis=1)


def _s5_kernel(x_ref, mod_ref, npre_ref, npost_ref, wb_ref, wcre_ref, wcim_ref, a_ref, d_ref,
               wglu_ref, bglu_ref, h0_ref, y_ref, sout_ref, htm, bu, ytm, otm, state, *, bt, t):
    nsub = bt // SUBLANES

    @pl.when(pl.program_id(0) == 0)
    def _():
        state[...] = h0_ref[...]

    for b in range(bt):
        hb = _rms(x_ref[b], npre_ref[...]) * (1.0 + mod_ref[1, b:b + 1, :]) + mod_ref[0, b:b + 1, :]
        _to_time_major(htm, b, hb, t, bt)

    for kb in range(S5_KBLOCKS):
        cs = slice(kb * MXU_DIM, (kb + 1) * MXU_DIM)
        hkb = _slab_cols(htm, kb * MXU_DIM, (kb + 1) * MXU_DIM)
        bu[...] = _dot(hkb.astype(BF16), wb_ref[kb])
        for lb0 in range(0, S5_SPB // LANES, S5_LANE_BLOCKS):
            for sb in range(nsub):
                cols = [(lb0 + i) * LANES for i in range(S5_LANE_BLOCKS)]
                rs = slice(sb * SUBLANES, (sb + 1) * SUBLANES)
                ar = [jnp.broadcast_to(a_ref[0:1, kb * S5_SPB + c:kb * S5_SPB + c + LANES], (SUBLANES, LANES))
                      for c in cols]
                ai = [jnp.broadcast_to(a_ref[1:2, kb * S5_SPB + c:kb * S5_SPB + c + LANES], (SUBLANES, LANES))
                      for c in cols]
                init = tuple(state[0, rs, kb * S5_SPB + c:kb * S5_SPB + c + LANES] for c in cols) + \
                    tuple(state[1, rs, kb * S5_SPB + c:kb * S5_SPB + c + LANES] for c in cols)

                def step(i, carry, cols=cols, ar=ar, ai=ai, sb=sb):
                    r0 = pl.multiple_of(i * bt + sb * SUBLANES, SUBLANES)
                    out = [None] * (2 * S5_LANE_BLOCKS)
                    for n, c in enumerate(cols):
                        sr, si = carry[n], carry[S5_LANE_BLOCKS + n]
                        nr = ar[n] * sr - ai[n] * si + bu[pl.ds(r0, SUBLANES), c:c + LANES]
                        ni = ar[n] * si + ai[n] * sr + bu[pl.ds(r0, SUBLANES), S5_SPB + c:S5_SPB + c + LANES]
                        bu[pl.ds(r0, SUBLANES), c:c + LANES] = nr
                        bu[pl.ds(r0, SUBLANES), S5_SPB + c:S5_SPB + c + LANES] = ni
                        out[n], out[S5_LANE_BLOCKS + n] = nr, ni
                    return tuple(out)

                fin = lax.fori_loop(0, t, step, init, unroll=min(t, 8))
                for n, c in enumerate(cols):
                    state[0, rs, kb * S5_SPB + c:kb * S5_SPB + c + LANES] = fin[n]
                    state[1, rs, kb * S5_SPB + c:kb * S5_SPB + c + LANES] = fin[S5_LANE_BLOCKS + n]
        ykb = _dot(bu[:, :S5_SPB].astype(BF16), wcre_ref[kb]) - _dot(bu[:, S5_SPB:].astype(BF16), wcim_ref[kb])
        ytm[:, cs] = ykb + d_ref[:, cs] * hkb

    g = _gelu(ytm[...])
    out = g * _sigmoid(_dot(g.astype(BF16), wglu_ref[...]) + bglu_ref[...])
    out = _rms(out, npost_ref[...])
    for c in range(otm.shape[0]):
        otm[c] = out[:, c * LANES:(c + 1) * LANES]
    for b in range(bt):
        y_ref[b] = x_ref[b] + mod_ref[2, b:b + 1, :] * _from_time_major(otm, b, t, bt)
    sout_ref[...] = state[...]


def _s5_layer(x, mod, npre, npost, wb, wcre, wcim, a, dskip, wglu, bglu, h0, t):
    bsz, seq, d = x.shape
    t = min(t, seq)
    rows = bsz * t
    kern = functools.partial(_s5_kernel, bt=bsz, t=t)
    full = lambda shape: pl.BlockSpec(shape, lambda i: (0,) * len(shape))
    return pl.pallas_call(
        kern,
        grid=(seq // t,),
        in_specs=[pl.BlockSpec((bsz, t, d), lambda i: (0, i, 0)),
                  full((3, bsz, d)), full((1, d)), full((1, d)),
                  full(wb.shape), full(wcre.shape), full(wcim.shape), full((2, N_STATE)), full((1, d)),
                  full((d, d)), full((1, d)), full((2, bsz, N_STATE))],
        out_specs=(pl.BlockSpec((bsz, t, d), lambda i: (0, i, 0)), full((2, bsz, N_STATE))),
        out_shape=(jax.ShapeDtypeStruct((bsz, seq, d), F32), jax.ShapeDtypeStruct((2, bsz, N_STATE), F32)),
        scratch_shapes=[pltpu.VMEM((d // LANES, rows, LANES), F32), pltpu.VMEM((rows, 2 * S5_SPB), F32),
                        pltpu.VMEM((rows, d), F32), pltpu.VMEM((d // LANES, rows, LANES), F32),
                        pltpu.VMEM((2, bsz, N_STATE), F32)],
        compiler_params=_cparams("arbitrary"),
        name="s5_layer",
    )(x, mod, npre, npost, wb, wcre, wcim, a, dskip, wglu, bglu, h0)


def _ffn_kernel(x_ref, mod_ref, npre_ref, npost_ref, wa_ref, wg_ref, cwa_ref, cwg_ref, cba_ref, cbg_ref,
                wd_ref, pfa_ref, pfg_ref, y_ref, sta_ref, stg_ref, htm, hbf, ea, eg, acc, cara, carg, *, bt, t):
    i, j = pl.program_id(0), pl.program_id(1)
    rows = bt * t
    halo = (CONV_WIDTH - 1) * bt

    @pl.when(j == 0)
    def _():
        for b in range(bt):
            hb = _rms(x_ref[b], npre_ref[...]) * (1.0 + mod_ref[1, b:b + 1, :]) + mod_ref[0, b:b + 1, :]
            _to_time_major(htm, b, hb, t, bt)
        hbf[...] = _slab_cols(htm, 0, D_MODEL).astype(BF16)
        acc[...] = jnp.zeros_like(acc)

    @pl.when(i == 0)
    def _():
        cara[j] = pfa_ref[...]
        carg[j] = pfg_ref[...]

    h = hbf[...]

    def conv(e, w_ref, car, cw_ref, cb_ref):
        e[0:halo, :] = car[j]
        e[halo:halo + rows, :] = _dot(h, w_ref[...])
        out = cb_ref[...] + cw_ref[0:1, :] * e[0:rows, :]
        for k in range(1, CONV_WIDTH):
            out = out + cw_ref[k:k + 1, :] * e[k * bt:k * bt + rows, :]
        car[j] = e[rows:rows + halo, :]
        return out

    a = conv(ea, wa_ref, cara, cwa_ref, cba_ref)
    g = conv(eg, wg_ref, carg, cwg_ref, cbg_ref)
    acc[...] += _dot((_silu(g) * a).astype(BF16), wd_ref[...])
    sta_ref[j] = cara[j]
    stg_ref[j] = carg[j]

    @pl.when(j == pl.num_programs(1) - 1)
    def _():
        out = _rms(acc[...], npost_ref[...])
        for c in range(htm.shape[0]):
            htm[c] = out[:, c * LANES:(c + 1) * LANES]
        for b in range(bt):
            y_ref[b] = x_ref[b] + mod_ref[2, b:b + 1, :] * _from_time_major(htm, b, t, bt)


def _ffn_layer(x, mod, npre, npost, wa, wg, cwa, cwg, cba, cbg, wd, pfa, pfg, t, tn):
    bsz, seq, d = x.shape
    t = min(t, seq)
    rows = bsz * t
    halo = (CONV_WIDTH - 1) * bsz
    nj = D_FF // tn
    kern = functools.partial(_ffn_kernel, bt=bsz, t=t)
    full = lambda shape: pl.BlockSpec(shape, lambda i, j: (0,) * len(shape))
    col = lambda r: pl.BlockSpec((r, tn), lambda i, j: (0, j))
    return pl.pallas_call(
        kern,
        grid=(seq // t, nj),
        in_specs=[pl.BlockSpec((bsz, t, d), lambda i, j: (0, i, 0)),
                  full((3, bsz, d)), full((1, d)), full((1, d)),
                  col(d), col(d), col(CONV_WIDTH), col(CONV_WIDTH), col(1), col(1),
                  pl.BlockSpec((tn, d), lambda i, j: (j, 0)), col(halo), col(halo)],
        out_specs=(pl.BlockSpec((bsz, t, d), lambda i, j: (0, i, 0)), full((nj, halo, tn)), full((nj, halo, tn))),
        out_shape=(jax.ShapeDtypeStruct((bsz, seq, d), F32), jax.ShapeDtypeStruct((nj, halo, tn), F32),
                   jax.ShapeDtypeStruct((nj, halo, tn), F32)),
        scratch_shapes=[pltpu.VMEM((d // LANES, rows, LANES), F32), pltpu.VMEM((rows, d), BF16),
                        pltpu.VMEM((rows + halo, tn), F32),
                        pltpu.VMEM((rows + halo, tn), F32), pltpu.VMEM((rows, d), F32),
                        pltpu.VMEM((nj, halo, tn), F32), pltpu.VMEM((nj, halo, tn), F32)],
        compiler_params=_cparams("arbitrary", "arbitrary"),
        name="conv_ffn",
    )(x, mod, npre, npost, wa, wg, cwa, cwg, cba, cbg, wd, pfa, pfg)


def _rope_tables(pos):
    half = HEAD_DIM // 2
    freqs = ROPE_THETA ** (-np.arange(half, dtype=np.float64) / half)
    ang = pos.astype(np.float64)[:, None] * freqs[None, :]
    cos = np.concatenate([np.cos(ang), np.cos(ang)], axis=1)
    sin = np.concatenate([-np.sin(ang), np.sin(ang)], axis=1)
    rep = LANES // HEAD_DIM
    return (jnp.asarray(np.tile(cos, (1, rep)), F32), jnp.asarray(np.tile(sin, (1, rep)), F32))


def _rope(x, cos, sin):
    half = HEAD_DIM // 2
    first = (lax.broadcasted_iota(jnp.int32, cos.shape, 1) % HEAD_DIM) < half
    out = []
    for c in range(x.shape[1] // LANES):
        s = x[:, c * LANES:(c + 1) * LANES]
        swapped = jnp.where(first, pltpu.roll(s, LANES - half, 1), pltpu.roll(s, half, 1))
        out.append(s * cos + swapped * sin)
    return jnp.concatenate(out, axis=1)


QG_PAD = N_HEADS * HEAD_DIM + LANES
NSA_TQ = 256
NSA_TK = 256
NSA_ROWS = 128
NSA_COLS = 256


def _proj_kernel(x_ref, mkv_ref, mq_ref, nkv_ref, nq_ref, wkv_ref, wqg_ref, cos_ref, sin_ref,
                 rows_ref, win_ref, kvb_ref, qp_ref, qr_ref, gate_ref, *rest, bt, t, transposed):
    if transposed:
        vt_ref, hkv, hq = rest
    else:
        hkv, hq = rest
    for b in range(bt):
        xb = x_ref[b]
        xn = xb * lax.rsqrt(jnp.mean(xb * xb, axis=-1, keepdims=True) + EPS)
        hkv[b * t:(b + 1) * t, :] = (xn * nkv_ref[...]) * (1.0 + mkv_ref[1, b]) + mkv_ref[0, b]
        hq[b * t:(b + 1) * t, :] = (xn * nq_ref[...]) * (1.0 + mq_ref[1, b]) + mq_ref[0, b]
    cos, sin = cos_ref[...], sin_ref[...]
    kv = _dot(hkv[...].astype(BF16), wkv_ref[...])
    w = KV_WIDTH
    ksel = _rope(kv[:, 2 * w:3 * w], cos, sin)
    kwin = _rope(kv[:, 4 * w:5 * w], cos, sin)
    rows = jnp.concatenate([kv[:, :2 * w], ksel, kv[:, 3 * w:4 * w]], axis=1)
    win = jnp.concatenate([kwin, kv[:, 5 * w:]], axis=1)
    qg = _dot(hq[...].astype(BF16), wqg_ref[...])
    q = qg[:, :N_HEADS * HEAD_DIM] * (HEAD_DIM ** -0.5)
    qrot = _rope(q, cos, sin)
    gates = _sigmoid(qg[:, N_HEADS * HEAD_DIM:])
    if not transposed:
        for b in range(bt):
            rs = slice(b * t, (b + 1) * t)
            rows_ref[b] = rows[rs]
            win_ref[b] = win[rs]
            kvb_ref[b] = jnp.concatenate([rows[rs, 2 * w:], win[rs]], axis=1).astype(BF16)
            qp_ref[b] = q[rs].astype(BF16)
            qr_ref[b] = qrot[rs].astype(BF16)
            gate_ref[b] = gates[rs]
        return
    rows_ref[0] = rows
    win_ref[0] = win
    kvb_ref[0] = jnp.concatenate([ksel, kwin], axis=1).astype(BF16)
    qp_ref[0] = (q * LOG2E).T.astype(BF16)
    qr_ref[0] = (qrot * LOG2E).T.astype(BF16)
    gate_ref[0] = gates.T
    ones = jnp.ones((HEAD_DIM, NSA_TK), F32)
    for j in range(t // NSA_TK):
        rs = slice(j * NSA_TK, (j + 1) * NSA_TK)
        vt = jnp.concatenate([kv[rs, 3 * w:4 * w], kv[rs, 5 * w:]], axis=1).T
        parts = []
        for h in range(2 * N_KV_HEADS):
            parts += [vt[h * HEAD_DIM:(h + 1) * HEAD_DIM], ones]
        vt_ref[0, j] = jnp.concatenate(parts, axis=0).astype(BF16)


def _proj(x, mkv, mq, nkv, nq, wkv, wqg, cos, sin, bt, t, transposed):
    bsz, seq, d = x.shape
    t = min(t, seq)
    r = bt * t
    kern = functools.partial(_proj_kernel, bt=bt, t=t, transposed=transposed)
    full = lambda shape: pl.BlockSpec(shape, lambda b, i: (0,) * len(shape))
    tile = lambda n: pl.BlockSpec((bt, t, n), lambda b, i: (b, i, 0))
    tile_t = lambda n: pl.BlockSpec((bt, n, t), lambda b, i: (b, 0, i))
    mod = lambda n: pl.BlockSpec((n, bt, 1, d), lambda b, i: (0, b, 0, 0))
    tab = pl.BlockSpec((r, LANES), lambda b, i: (i, 0))
    nq_cols = N_HEADS * HEAD_DIM
    sds = jax.ShapeDtypeStruct
    wlen = min(WINDOW, seq)
    assert wlen % t == 0
    first_kept = (seq - wlen) // t
    win_tile = pl.BlockSpec((bt, t, 2 * KV_WIDTH), lambda b, i: (b, jnp.maximum(i - first_kept, 0), 0))
    out_specs = [tile(4 * KV_WIDTH), win_tile]
    out_shape = [sds((bsz, seq, 4 * KV_WIDTH), F32), sds((bsz, wlen, 2 * KV_WIDTH), F32)]
    if transposed:
        assert bt == 1 and t % NSA_TK == 0
        vrows = 2 * N_KV_HEADS * LANES
        out_specs += [tile(2 * KV_WIDTH), tile_t(nq_cols), tile_t(nq_cols), tile_t(LANES),
                      pl.BlockSpec((1, t // NSA_TK, vrows, NSA_TK), lambda b, i: (b, i, 0, 0))]
        out_shape += [sds((bsz, seq, 2 * KV_WIDTH), BF16), sds((bsz, nq_cols, seq), BF16),
                      sds((bsz, nq_cols, seq), BF16), sds((bsz, LANES, seq), F32),
                      sds((bsz, seq // NSA_TK, vrows, NSA_TK), BF16)]
    else:
        out_specs += [tile(4 * KV_WIDTH), tile(nq_cols), tile(nq_cols), tile(LANES)]
        out_shape += [sds((bsz, seq, 4 * KV_WIDTH), BF16), sds((bsz, seq, nq_cols), BF16),
                      sds((bsz, seq, nq_cols), BF16), sds((bsz, seq, LANES), F32)]
    return pl.pallas_call(
        kern,
        grid=(bsz // bt, seq // t),
        in_specs=[tile(d), mod(2), mod(3), full((1, d)), full((1, d)), full(wkv.shape), full(wqg.shape),
                  tab, tab],
        out_specs=tuple(out_specs),
        out_shape=tuple(out_shape),
        scratch_shapes=[pltpu.VMEM((r, d), F32), pltpu.VMEM((r, d), F32)],
        compiler_params=_cparams("parallel", "arbitrary"),
        name="kv_q_proj",
    )(x, mkv, mq, nkv, nq, wkv, wqg, cos, sin)


PAGE_SIZE = 128
CHUNKS_PER_PAGE = PAGE_SIZE // CMP_STRIDE
HEADS_PER_SLAB = LANES // HEAD_DIM
CMP_SLABS = 2 * KV_WIDTH // LANES


def _compress_kernel(tbl_ref, *refs, n_pages, transposed):
    page_refs = refs[:n_pages]
    pe_ref, w1_ref, w1bd_ref, b1_ref, w2_ref, perm_ref, out_ref, outt_ref, zs, xs = refs[n_pages:]
    m = n_pages * CHUNKS_PER_PAGE
    for k, pr in enumerate(page_refs):
        zs[k * LANES:(k + 1) * LANES, :] = (pr[0] if transposed else pr[0].T).astype(BF16)
    zp = _dot(zs[...], perm_ref[...])
    for k in range(n_pages):
        xs[k] = zp[k * LANES:(k + 1) * LANES, :].T
    flat = jnp.concatenate([xs[:, l * CHUNKS_PER_PAGE:(l + 1) * CHUNKS_PER_PAGE, :].reshape(m, LANES)
                            for l in range(CMP_STRIDE)], axis=1)
    pb = jnp.dot(pe_ref[0], w1_ref[0], preferred_element_type=F32, precision=lax.Precision.HIGHEST)
    bias = pb[0:1, :CMP_HIDDEN] + pb[1:2, CMP_HIDDEN:] + b1_ref[0]
    pab2 = _dot(flat.astype(BF16), w1bd_ref[0])
    hid = []
    for h in range(HEADS_PER_SLAB):
        pab = pab2[:, h * 2 * CMP_HIDDEN:(h + 1) * 2 * CMP_HIDDEN]
        nxt = pltpu.roll(pab[:, CMP_HIDDEN:], m - 1, 0)
        hid.append(_gelu(pab[:, :CMP_HIDDEN] + nxt + bias))
    out = _dot(jnp.concatenate(hid, axis=1).astype(BF16), w2_ref[0])
    out_ref[0] = out.astype(out_ref.dtype)
    outt_ref[0] = out.T.astype(outt_ref.dtype)


def _compress(src, table, n_pages, pe2, w1f, w1bd, b1, w2bd, transposed):
    bsz = table.shape[0] // n_pages
    m = n_pages * CHUNKS_PER_PAGE
    kern = functools.partial(_compress_kernel, n_pages=n_pages, transposed=transposed)
    slot = lambda s: s // (CMP_SLABS // 2)
    if transposed:
        page = lambda k: pl.BlockSpec((1, LANES, PAGE_SIZE), lambda b, s, tbl: (tbl[b * n_pages + k], s, 0))
    else:
        page = lambda k: pl.BlockSpec((1, PAGE_SIZE, LANES), lambda b, s, tbl: (tbl[b * n_pages + k], 0, s))
    wspec = lambda shape: pl.BlockSpec((1,) + shape, lambda b, s, tbl: (slot(s), 0, 0))
    r = np.arange(PAGE_SIZE)
    perm = jnp.asarray((r % CMP_STRIDE * CHUNKS_PER_PAGE + r // CMP_STRIDE)[:, None] == r[None, :], BF16)
    return pl.pallas_call(
        kern,
        grid_spec=pltpu.PrefetchScalarGridSpec(
            num_scalar_prefetch=1,
            grid=(bsz, CMP_SLABS),
            in_specs=[page(k) for k in range(n_pages)] + [
                wspec(pe2.shape[1:]), wspec(w1f.shape[1:]), wspec(w1bd.shape[1:]), wspec(b1.shape[1:]),
                wspec(w2bd.shape[1:]), pl.BlockSpec(perm.shape, lambda b, s, tbl: (0, 0))],
            out_specs=(pl.BlockSpec((1, m, LANES), lambda b, s, tbl: (b, 0, s)),
                       pl.BlockSpec((1, LANES, m), lambda b, s, tbl: (b, s, 0))),
            scratch_shapes=[pltpu.VMEM((n_pages * LANES, PAGE_SIZE), BF16),
                            pltpu.VMEM((n_pages, PAGE_SIZE, LANES), F32)],
        ),
        out_shape=(jax.ShapeDtypeStruct((bsz, m, 2 * KV_WIDTH), BF16),
                   jax.ShapeDtypeStruct((bsz, 2 * KV_WIDTH, m), BF16)),
        compiler_params=_cparams("parallel", "parallel"),
        name="compress_%d" % n_pages,
    )(table, *([src] * n_pages), pe2, w1f, w1bd, b1, w2bd, perm)


def _sel_matrix(nc, ns, rows, cols):
    r, q = SEL_BLOCK // CMP_STRIDE, CMP_BLOCK // CMP_STRIDE
    offs = (np.arange(r)[:, None] - np.arange(q)[None, :]).reshape(-1)
    target = r * np.arange(ns)[None, :, None] + offs[None, None, :]
    m = (np.arange(nc)[:, None, None] == target).sum(-1).astype(np.float32)
    out = np.zeros((rows, cols), np.float32)
    out[:nc, :ns] = m
    return out


def _select_mask_t(score_t, qpos, ns, n_sel):
    j = lax.broadcasted_iota(jnp.int32, score_t.shape, 0)
    cur = qpos // SEL_BLOCK
    forced = (j == 0) | (j == cur) | (j == cur - 1)
    valid = j * SEL_BLOCK <= qpos
    s = jnp.where(forced, jnp.inf, jnp.where(valid, score_t, -jnp.inf))
    s = jnp.where(j < ns, s, -jnp.inf)
    rank = jnp.zeros(score_t.shape, jnp.int32)
    for jp in range(ns):
        sp = s[jp:jp + 1, :]
        ahead = (sp > s) | ((sp == s) & (jp < j))
        rank = rank + ahead.astype(jnp.int32)
    return ((rank < n_sel) & (j < ns)).astype(F32)


def _nsa_prompt_kernel(x_ref, qpt_ref, qrt_ref, gatet_ref, cmp_ref, cmpt_ref, kk_ref, vt_ref, mod_ref, npost_ref,
                       wot_ref, mselt_ref, expandt_ref, y_ref, bias_sel, bias_win, q4t_s, m_s, a_s, acc_s, s_s, p_s,
                       ot_s, *, tq, tk, seq, ns):
    qt = pl.program_id(1)
    t0 = qt * tq
    w = KV_WIDTH
    grp = GQA_GROUP
    nq = grp * tq
    nc_pad = cmp_ref.shape[1]
    qpos_row = t0 + lax.broadcasted_iota(jnp.int32, (1, tq), 1)
    cends = lax.broadcasted_iota(jnp.int32, (nc_pad, 1), 0) * CMP_STRIDE + (CMP_BLOCK - 1)
    cvalid = cends <= qpos_row
    cbias = jnp.concatenate([jnp.where(cvalid, 0.0, NEG)] * grp, axis=1)
    cmask = jnp.concatenate([cvalid.astype(F32)] * grp, axis=1)
    n_sel_tiles = (t0 + tq + tk - 1) // tk
    n_win_tiles = (WINDOW + tq + tk - 1) // tk

    for i in range(n_win_tiles):
        kpos = t0 + tq - (i + 1) * tk + lax.broadcasted_iota(jnp.int32, (tk, 1), 0)
        diff = qpos_row - kpos
        bias_win[i] = jnp.where((diff >= 0) & (diff < WINDOW) & (kpos >= 0), 0.0, NEG)
    kpos_diag = t0 + lax.broadcasted_iota(jnp.int32, (tq, 1), 0)

    def attend(kcol, vrow, n_tiles, unroll, rem_step, start_of, bias_of):
        m_s[...] = jnp.full(m_s.shape, NEG, F32)
        acc_s[...] = jnp.zeros(acc_s.shape, F32)

        def tile(i, u):
            k0 = pl.multiple_of(start_of(i), tk)
            k = kk_ref[0, pl.ds(k0, tk), kcol:kcol + HEAD_DIM]
            vt = vt_ref[0, k0 // tk, vrow:vrow + LANES, :]
            s_s[u] = _dot(k, q4t_s[...])
            for c in range(nq // NSA_COLS):
                cols = slice(c * NSA_COLS, (c + 1) * NSA_COLS)
                top = None
                for r in range(tk // NSA_ROWS):
                    rows = slice(r * NSA_ROWS, (r + 1) * NSA_ROWS)
                    s = s_s[u, rows, cols] + bias_of(i, k0, r * NSA_ROWS, (c * NSA_COLS) % tq)
                    s_s[u, rows, cols] = s
                    cur = jnp.max(s, axis=0, keepdims=True)
                    top = cur if top is None else jnp.maximum(top, cur)
                m_old = m_s[:, cols]
                m_new = jnp.maximum(m_old, top)
                m_rep = jnp.concatenate([m_new] * (NSA_ROWS // SUBLANES), axis=0)
                for r in range(tk // NSA_ROWS):
                    rows = slice(r * NSA_ROWS, (r + 1) * NSA_ROWS)
                    p_s[u, rows, cols] = jnp.exp2(s_s[u, rows, cols] - m_rep).astype(BF16)
                a_s[u, :, cols] = jnp.exp2(m_old - m_new)
                m_s[:, cols] = m_new
            alpha = jnp.concatenate([a_s[u]] * (LANES // SUBLANES), axis=0)
            acc_s[...] = alpha * acc_s[...] + _dot(vt, p_s[u])

        def body(j, carry):
            for u in range(unroll):
                tile(j * unroll + u, u)
            return carry

        n_full = n_tiles // unroll
        lax.fori_loop(0, n_full, body, 0)
        if rem_step:
            @pl.when(n_tiles - n_full * unroll >= rem_step)
            def _():
                for u in range(rem_step):
                    tile(n_full * unroll + u, u)
        return acc_s[:HEAD_DIM, :] / acc_s[HEAD_DIM:, :]

    for kh in range(N_KV_HEADS):
        heads = [kh * grp + g for g in range(grp)]
        for g, h in enumerate(heads):
            q4t_s[:, g * tq:(g + 1) * tq] = qpt_ref[0, h * HEAD_DIM:(h + 1) * HEAD_DIM, :]
        s = _dot(cmp_ref[0, :, kh * HEAD_DIM:(kh + 1) * HEAD_DIM], q4t_s[...]) + cbias
        e = jnp.exp2(s - jnp.max(s, axis=0, keepdims=True))
        p = e / jnp.sum(e, axis=0, keepdims=True) * cmask
        o_cmp = _dot(cmpt_ref[0, w + kh * HEAD_DIM:w + (kh + 1) * HEAD_DIM, :], p.astype(BF16))
        psum = p[:, 0:tq]
        for g in range(1, grp):
            psum = psum + p[:, g * tq:(g + 1) * tq]
        score_t = jnp.dot(mselt_ref[...], psum, preferred_element_type=F32, precision=lax.Precision.HIGHEST)
        sel_t = _select_mask_t(score_t, qpos_row, ns, N_SEL)
        bias_sel[...] = _dot(expandt_ref[...], jnp.where(sel_t > 0.5, 0.0, NEG).astype(BF16))
        diag = pl.ds(pl.multiple_of(t0, tq), tq)
        bias_sel[diag, :] = jnp.where(kpos_diag <= qpos_row, bias_sel[diag, :], NEG)

        for g, h in enumerate(heads):
            q4t_s[:, g * tq:(g + 1) * tq] = qrt_ref[0, h * HEAD_DIM:(h + 1) * HEAD_DIM, :]
        o_sel = attend(kh * HEAD_DIM, kh * LANES, n_sel_tiles, 2 * (tq // tk), tq // tk, lambda i: i * tk,
                       lambda i, k0, r0, c0: bias_sel[pl.ds(k0 + r0, NSA_ROWS), c0:c0 + NSA_COLS])
        o_win = attend(w + kh * HEAD_DIM, (N_KV_HEADS + kh) * LANES, n_win_tiles, n_win_tiles, 0,
                       lambda i: jnp.maximum(t0 + tq - (i + 1) * tk, 0),
                       lambda i, k0, r0, c0: bias_win[i, r0:r0 + NSA_ROWS, c0:c0 + NSA_COLS])
        for g, h in enumerate(heads):
            cs = slice(g * tq, (g + 1) * tq)
            gc = gatet_ref[0, h:h + 1, :]
            gs = gatet_ref[0, N_HEADS + h:N_HEADS + h + 1, :]
            gw = gatet_ref[0, 2 * N_HEADS + h:2 * N_HEADS + h + 1, :]
            ot_s[h * HEAD_DIM:(h + 1) * HEAD_DIM, :] = (
                gc * o_cmp[:, cs] + gs * o_sel[:, cs] + gw * o_win[:, cs]).astype(BF16)

    out = _dot(wot_ref[...], ot_s[...]).T
    y_ref[0] = x_ref[0] + mod_ref[0] * _rms(out, npost_ref[...])


def _nsa_prompt(x, qpt, qrt, gatet, cmp, cmpt, kk, vt, gate_mod, npost, wot, tq, tk):
    bsz, seq, d = x.shape
    tq = min(tq, seq)
    assert tq % tk == 0 and seq % tq == 0 and tq % NSA_COLS == 0 and vt.shape[3] == tk
    nc = (seq - CMP_BLOCK) // CMP_STRIDE + 1
    ns = -(-seq // SEL_BLOCK)
    nc_pad = cmp.shape[1]
    j_pad = -(-ns // 16) * 16
    n_win = (WINDOW + tq + tk - 1) // tk
    mselt = jnp.asarray(_sel_matrix(nc, ns, nc_pad, j_pad).T)
    expandt = jnp.asarray((np.arange(seq)[:, None] // SEL_BLOCK == np.arange(j_pad)[None, :]), BF16)
    kern = functools.partial(_nsa_prompt_kernel, tq=tq, tk=tk, seq=seq, ns=ns)
    full = lambda shape: pl.BlockSpec(shape, lambda b, i: (0,) * len(shape))
    tile = lambda n: pl.BlockSpec((1, tq, n), lambda b, i: (b, i, 0))
    tile_t = lambda n: pl.BlockSpec((1, n, tq), lambda b, i: (b, 0, i))
    per_b = lambda *shape: pl.BlockSpec((1,) + shape, lambda b, i: (b,) + (0,) * len(shape))
    nq = GQA_GROUP * tq
    return pl.pallas_call(
        kern,
        grid=(bsz, seq // tq),
        in_specs=[tile(d), tile_t(qpt.shape[1]), tile_t(qrt.shape[1]), tile_t(LANES), per_b(*cmp.shape[1:]),
                  per_b(*cmpt.shape[1:]), per_b(*kk.shape[1:]), per_b(*vt.shape[1:]), per_b(1, d), full((1, d)),
                  full(wot.shape), full(mselt.shape), full(expandt.shape)],
        out_specs=tile(d),
        out_shape=jax.ShapeDtypeStruct((bsz, seq, d), F32),
        scratch_shapes=[pltpu.VMEM((seq, tq), F32), pltpu.VMEM((n_win, tk, tq), F32),
                        pltpu.VMEM((HEAD_DIM, nq), BF16), pltpu.VMEM((SUBLANES, nq), F32),
                        pltpu.VMEM((n_win, SUBLANES, nq), F32), pltpu.VMEM((LANES, nq), F32),
                        pltpu.VMEM((n_win, tk, nq), F32), pltpu.VMEM((n_win, tk, nq), BF16),
                        pltpu.VMEM((d, tq), BF16)],
        compiler_params=_cparams("parallel", "arbitrary"),
        name="nsa_prompt",
    )(x, qpt, qrt, gatet, cmp, cmpt, kk, vt, gate_mod, npost, wot, mselt, expandt)


def _nsa_sample_kernel(tbl_ref, *refs, n_pages, pages_per_step, past_len, n_steps, ns):
    page_refs = refs[:pages_per_step]
    (qp_ref, qr_ref, gate_ref, cmp_ref, new_ref, wc_ref, mselt_ref, expand_ref, expand_new_ref, out_ref,
     selt, m_s, l_s, acc_s, ocmp_s) = refs[pages_per_step:]
    pc = pl.program_id(1)
    w = KV_WIDTH
    nrow = qp_ref.shape[1]
    step_of_row = lax.broadcasted_iota(jnp.int32, (nrow, 1), 0) % n_steps
    qpos_col = past_len + step_of_row
    qr = qr_ref[0]

    @pl.when(pc == 0)
    def _():
        nc_pad = cmp_ref.shape[1]
        cends = lax.broadcasted_iota(jnp.int32, (1, nc_pad), 1) * CMP_STRIDE + (CMP_BLOCK - 1)
        cvalid = cends <= qpos_col
        s = jnp.where(cvalid, _dot_nt(qp_ref[0], cmp_ref[0, :, :w]), NEG)
        e = jnp.exp(s - jnp.max(s, axis=-1, keepdims=True))
        p = jnp.where(cvalid, e / jnp.sum(e, axis=-1, keepdims=True), 0.0)
        ocmp_s[...] = _dot(p.astype(BF16), cmp_ref[0, :, w:])
        grp = nrow // GQA_GROUP
        psum = p[0:grp]
        for g in range(1, GQA_GROUP):
            psum = psum + p[g * grp:(g + 1) * grp]
        psum = jnp.concatenate([psum] * GQA_GROUP, axis=0)
        score_t = lax.dot_general(mselt_ref[...], psum, (((1,), (1,)), ((), ())),
                                  preferred_element_type=F32, precision=lax.Precision.HIGHEST)
        qpos_row = past_len + lax.broadcasted_iota(jnp.int32, (1, nrow), 1) % n_steps
        selt[...] = _select_mask_t(score_t, qpos_row, ns, N_SEL).astype(BF16)
        m_s[...] = jnp.full(m_s.shape, NEG, F32)
        l_s[...] = jnp.zeros(l_s.shape, F32)
        acc_s[...] = jnp.zeros(acc_s.shape, F32)

    def flash(s, allowed, pv):
        s = jnp.where(allowed, s, NEG)
        m_old = m_s[...]
        m_new = jnp.maximum(m_old, jnp.max(s, axis=-1, keepdims=True))
        alpha = jnp.exp(m_old - m_new)
        p = jnp.exp(s - m_new)
        l_s[...] = alpha * l_s[...] + jnp.sum(p, axis=-1, keepdims=True)
        acc_s[...] = alpha * acc_s[...] + pv(p.astype(BF16))
        m_s[...] = m_new

    kt = jnp.concatenate([pr[0, :w, :] for pr in page_refs], axis=1).astype(BF16)
    vt = jnp.concatenate([pr[0, w:, :] for pr in page_refs], axis=1).astype(BF16)
    allowed = lax.dot_general(selt[...], expand_ref[...], (((0,), (0,)), ((), ())),
                              preferred_element_type=F32) > 0.5
    flash(_dot(qr, kt), allowed, lambda p: _dot_nt(p, vt))

    @pl.when(pc == pl.num_programs(1) - 1)
    def _():
        pad = jnp.zeros((LANES - n_steps, 4 * w), F32)
        new = jnp.concatenate([new_ref[0].astype(F32), pad], axis=0).astype(BF16)
        key_step = lax.broadcasted_iota(jnp.int32, (1, LANES), 1)
        causal_new = key_step <= step_of_row
        picked = lax.dot_general(selt[...], expand_new_ref[...], (((0,), (0,)), ((), ())),
                                 preferred_element_type=F32) > 0.5
        flash(_dot_nt(qr, new[:, :w]), causal_new & picked, lambda p: _dot(p, new[:, w:2 * w]))
        o_sel = acc_s[...] / l_s[...]

        w_buf = wc_ref.shape[2]
        buf_idx = lax.broadcasted_iota(jnp.int32, (1, w_buf), 1)
        diff_buf = step_of_row + w_buf - buf_idx
        s_buf = jnp.where(diff_buf < WINDOW, _dot(qr, wc_ref[0, :w, :].astype(BF16)), NEG)
        s_new = jnp.where(causal_new, _dot_nt(qr, new[:, 2 * w:3 * w]), NEG)
        sw = jnp.concatenate([s_buf, s_new], axis=1)
        ew = jnp.exp(sw - jnp.max(sw, axis=-1, keepdims=True))
        pw = (ew / jnp.sum(ew, axis=-1, keepdims=True)).astype(BF16)
        o_win = _dot_nt(pw[:, :w_buf], wc_ref[0, w:, :].astype(BF16)) + _dot(pw[:, w_buf:], new[:, 3 * w:])
        g = gate_ref[0]
        out_ref[0] = g[:, 0:1] * ocmp_s[...] + g[:, 1:2] * o_sel + g[:, 2:3] * o_win


SAMPLE_PAGES_PER_STEP = 16


def _nsa_sample(pool, table, n_pages, qp_bd, qr_bd, gates_r, cmp, new_kvb, wcache, past_len):
    bsz, nrow, _ = qp_bd.shape
    n_steps = new_kvb.shape[1]
    pps = min(SAMPLE_PAGES_PER_STEP, n_pages)
    nc = (past_len + n_steps - CMP_BLOCK) // CMP_STRIDE + 1
    ns = -(-(past_len + n_steps) // SEL_BLOCK)
    assert ns == past_len // SEL_BLOCK + 1 and n_steps <= SEL_BLOCK
    nc_pad = cmp.shape[1]
    j_pad = -(-ns // 16) * 16
    mselt = jnp.asarray(_sel_matrix(nc, ns, nc_pad, j_pad).T)
    expand = jnp.asarray((np.arange(j_pad)[:, None] == np.arange(past_len)[None, :] // SEL_BLOCK), BF16)
    expand_new = jnp.asarray(np.broadcast_to(np.arange(j_pad)[:, None] == ns - 1, (j_pad, LANES)), BF16)
    kern = functools.partial(_nsa_sample_kernel, n_pages=n_pages, pages_per_step=pps, past_len=past_len,
                             n_steps=n_steps, ns=ns)
    sel_rows = 1
    page = lambda k: pl.BlockSpec((1, 2 * KV_WIDTH, PAGE_SIZE),
                                  lambda b, c, tbl: (tbl[b * n_pages + c * pps + k], sel_rows, 0))
    per_b = lambda r, n: pl.BlockSpec((1, r, n), lambda b, c, tbl: (b, 0, 0))
    full = lambda shape: pl.BlockSpec(shape, lambda b, c, tbl: (0,) * len(shape))
    return pl.pallas_call(
        kern,
        grid_spec=pltpu.PrefetchScalarGridSpec(
            num_scalar_prefetch=1,
            grid=(bsz, n_pages // pps),
            in_specs=[page(k) for k in range(pps)] + [
                per_b(nrow, KV_WIDTH), per_b(nrow, KV_WIDTH), per_b(nrow, LANES), per_b(nc_pad, 2 * KV_WIDTH),
                per_b(n_steps, 4 * KV_WIDTH), per_b(2 * KV_WIDTH, wcache.shape[2]), full(mselt.shape),
                pl.BlockSpec((j_pad, pps * PAGE_SIZE), lambda b, c, tbl: (0, c)), full(expand_new.shape)],
            out_specs=per_b(nrow, KV_WIDTH),
            scratch_shapes=[pltpu.VMEM((j_pad, nrow), BF16), pltpu.VMEM((nrow, 1), F32),
                            pltpu.VMEM((nrow, 1), F32), pltpu.VMEM((nrow, KV_WIDTH), F32),
                            pltpu.VMEM((nrow, KV_WIDTH), F32)],
        ),
        out_shape=jax.ShapeDtypeStruct((bsz, nrow, KV_WIDTH), F32),
        compiler_params=_cparams("parallel", "arbitrary"),
        name="nsa_sample",
    )(table, *([pool] * pps), qp_bd, qr_bd, gates_r, cmp, new_kvb, wcache, mselt, expand, expand_new)


def _oproj_kernel(o_ref, x_ref, gate_ref, npost_ref, wo_ref, y_ref):
    out = _dot(o_ref[...].astype(BF16), wo_ref[...])
    y_ref[...] = x_ref[...] + gate_ref[...] * _rms(out, npost_ref[...])


def _oproj(o, x, gate, npost, wo):
    return pl.pallas_call(
        _oproj_kernel,
        out_shape=jax.ShapeDtypeStruct(x.shape, F32),
        compiler_params=pltpu.CompilerParams(vmem_limit_bytes=VMEM_LIMIT),
        name="nsa_out_proj",
    )(o, x, gate, npost, wo)


S5_T = 64
FFN_T = 64
FFN_TN = D_FF


def _prepare(p):
    w = {}
    a_re, a_im, zb_re, zb_im = _s5_prep(p['ssm_lam_re'][0], p['ssm_lam_im'][0], p['ssm_log_dt'][0],
                                        p['ssm_b_re'][0], p['ssm_b_im'][0])
    w['s5_a'] = jnp.stack([a_re.reshape(-1), a_im.reshape(-1)])
    w['s5_wb'] = jnp.concatenate([_s5_block_diag_in(zb_re), _s5_block_diag_in(zb_im)], axis=-1).astype(BF16)
    w['s5_wcre'] = _s5_block_diag_out(p['ssm_c_re'][0]).astype(BF16)
    w['s5_wcim'] = _s5_block_diag_out(p['ssm_c_im'][0]).astype(BF16)
    w['w_glu'] = p['w_glu'][0].astype(BF16)
    w['ffn_wa'] = p['ffn_w_up'][:, :, :D_FF].astype(BF16)
    w['ffn_wg'] = p['ffn_w_up'][:, :, D_FF:].astype(BF16)
    w['ffn_wd'] = p['ffn_w_down'].astype(BF16)
    w['w_kv'] = p['w_kv'].astype(BF16)
    w['w_qg'] = jnp.pad(p['w_qg'][0], ((0, 0), (0, QG_PAD - p['w_qg'].shape[2]))).astype(BF16)
    w['w_o'] = p['w_o'][0].astype(BF16)
    w['w_o_t'] = p['w_o'][0].T.astype(BF16)
    half = CMP_BLOCK // 2
    w1r = p['cmp_w1'].reshape(2, CMP_BLOCK, HEAD_DIM, CMP_HIDDEN)
    w['cmp_w1'] = jnp.concatenate([w1r[:, :half], w1r[:, half:]], axis=-1).reshape(
        2, half * HEAD_DIM, 2 * CMP_HIDDEN)
    w['cmp_pe'] = p['cmp_pe'].reshape(2, 2, half * HEAD_DIM)
    w['cmp_b1'] = p['cmp_b1'].reshape(2, 1, CMP_HIDDEN)
    eye = jnp.eye(HEADS_PER_SLAB, dtype=F32)
    w['cmp_w1bd'] = jnp.einsum('sldn,ab->sladbn', w['cmp_w1'].reshape(2, half, HEAD_DIM, 2 * CMP_HIDDEN),
                               eye).reshape(2, half * LANES, HEADS_PER_SLAB * 2 * CMP_HIDDEN).astype(BF16)
    w['cmp_w2'] = jnp.einsum('shd,ab->sahbd', p['cmp_w2'], eye).reshape(
        2, HEADS_PER_SLAB * CMP_HIDDEN, LANES).astype(BF16)
    return w


def _split3(m):
    return m.reshape(m.shape[0], 3, D_MODEL).transpose(1, 0, 2)


def _ffn_call(x, mod, layer, p, w, prefix):
    bsz = x.shape[0]
    halo = (CONV_WIDTH - 1) * bsz
    pf = prefix.transpose(1, 0, 2).reshape(halo, 2 * D_FF)
    cw, cb = p['ffn_conv_w'][layer], p['ffn_conv_b'][layer].reshape(1, -1)
    y, sta, stg = _ffn_layer(x, mod, p['norm_pre'][layer, 1:2], p['norm_post'][layer, 1:2],
                             w['ffn_wa'][layer], w['ffn_wg'][layer], cw[:, :D_FF], cw[:, D_FF:],
                             cb[:, :D_FF], cb[:, D_FF:], w['ffn_wd'][layer], pf[:, :D_FF], pf[:, D_FF:],
                             FFN_T, FFN_TN)
    st = jnp.stack([sta, stg]).reshape(2, D_FF // FFN_TN, CONV_WIDTH - 1, bsz, FFN_TN)
    st = st.transpose(3, 2, 0, 1, 4).reshape(bsz, CONV_WIDTH - 1, 2 * D_FF)
    return y, st


def _layer0(x, mods, p, w, h0, prefix):
    bsz = x.shape[0]
    y, s = _s5_layer(x, _split3(mods[0]), p['norm_pre'][0, 0:1], p['norm_post'][0, 0:1], w['s5_wb'],
                     w['s5_wcre'], w['s5_wcim'], w['s5_a'], p['ssm_d'][0:1], w['w_glu'], p['b_glu'][0:1],
                     h0, S5_T)
    s = s.reshape(2, bsz, N_GROUPS, SSM_STATE)
    y, st = _ffn_call(y, _split3(mods[1]), 0, p, w, prefix)
    return y, s[0], s[1], st


PROJ_T = 512


def _proj_call(x, mods, modkv, p, w, pos, bt):
    bsz, seq, _ = x.shape
    cos, sin = _rope_tables(pos)
    if bt > 1:
        cos, sin = jnp.tile(cos, (bt, 1)), jnp.tile(sin, (bt, 1))
    mkv = modkv.reshape(bsz, 2, 1, D_MODEL).transpose(1, 0, 2, 3)
    mq = _split3(mods[2]).reshape(3, bsz, 1, D_MODEL)
    return _proj(x, mkv, mq, p['norm_kv'].reshape(1, D_MODEL), p['norm_pre'][1, 0:1], w['w_kv'], w['w_qg'],
                 cos, sin, bt, PROJ_T, bt == 1)


def _prompt_group(x, mods, modkv, p, w):
    bsz, seq, _ = x.shape
    zeros_state = jnp.zeros((2, bsz, N_STATE), F32)
    zeros_prefix = jnp.zeros((bsz, CONV_WIDTH - 1, 2 * D_FF), F32)
    x1, s_re, s_im, conv0 = _layer0(x, mods, p, w, zeros_state, zeros_prefix)
    rows, win, kk, qpt, qrt, gatet, vt = _proj_call(x1, mods, modkv, p, w, np.arange(seq), 1)
    n_pages = seq // PAGE_SIZE
    table = jnp.arange(bsz * n_pages, dtype=jnp.int32)
    cmp, cmpt = _compress(rows.reshape(bsz * n_pages, PAGE_SIZE, 4 * KV_WIDTH), table, n_pages,
                          w['cmp_pe'], w['cmp_w1'], w['cmp_w1bd'], w['cmp_b1'], w['cmp_w2'], False)
    gate_mod = _split3(mods[2])[2].reshape(bsz, 1, D_MODEL)
    x2 = _nsa_prompt(x1, qpt, qrt, gatet, cmp, cmpt, kk, vt, gate_mod, p['norm_post'][1, 0:1], w['w_o_t'],
                     NSA_TQ, NSA_TK)
    y, conv1 = _ffn_call(x2, _split3(mods[3]), 1, p, w, zeros_prefix)
    kvd = (N_KV_HEADS, HEAD_DIM)
    return (y, rows.reshape(bsz, seq, 4, *kvd), win.reshape(bsz, -1, 2, *kvd),
            s_re[None], s_im[None], jnp.stack([conv0, conv1]))


def _block_diag_rows(q):
    bsz, steps, _ = q.shape
    q5 = q.reshape(bsz, steps, N_KV_HEADS, GQA_GROUP, HEAD_DIM)
    eye = jnp.eye(N_KV_HEADS, dtype=q.dtype)
    return jnp.einsum('bqkgd,kj->bgkqjd', q5, eye).reshape(bsz, GQA_GROUP * N_KV_HEADS * steps, KV_WIDTH)


def _sample_group(x, mods, modkv, p, w, pool, page_table, wcache, ssm_re, ssm_im, conv_state):
    bsz, steps, _ = x.shape
    n_pages = page_table.shape[1]
    past_len = n_pages * PAGE_SIZE
    h0 = jnp.stack([ssm_re[0].reshape(bsz, N_STATE), ssm_im[0].reshape(bsz, N_STATE)])
    x1, s_re, s_im, conv0 = _layer0(x, mods, p, w, h0, conv_state[0])
    rows, win, kvb, qp, qr, gates = _proj_call(x1, mods, modkv, p, w, past_len + np.arange(steps), bsz)
    pool2 = pool.transpose(0, 2, 3, 4, 1).reshape(pool.shape[0], 4 * KV_WIDTH, PAGE_SIZE)
    wcache2 = wcache.transpose(0, 2, 3, 4, 1).reshape(bsz, 2 * KV_WIDTH, wcache.shape[1])
    table = page_table.reshape(-1)
    cmp, _ = _compress(pool2, table, n_pages, w['cmp_pe'], w['cmp_w1'], w['cmp_w1bd'], w['cmp_b1'], w['cmp_w2'],
                       True)
    nrow = GQA_GROUP * N_KV_HEADS * steps
    g5 = gates[:, :, :3 * N_HEADS].reshape(bsz, steps, 3, N_KV_HEADS, GQA_GROUP)
    gates_r = jnp.pad(g5.transpose(0, 4, 3, 1, 2).reshape(bsz, nrow, 3), ((0, 0), (0, 0), (0, LANES - 3)))
    o_bd = _nsa_sample(pool2, table, n_pages, _block_diag_rows(qp), _block_diag_rows(qr), gates_r, cmp, kvb,
                       wcache2, past_len)
    o6 = o_bd.reshape(bsz, GQA_GROUP, N_KV_HEADS, steps, N_KV_HEADS, HEAD_DIM)
    o = jnp.einsum('bgkqjd,kj->bqkgd', o6, jnp.eye(N_KV_HEADS, dtype=F32)).reshape(bsz * steps, D_MODEL)
    gate_mod = jnp.repeat(_split3(mods[2])[2], steps, axis=0)
    x2 = _oproj(o, x1.reshape(bsz * steps, D_MODEL), gate_mod, p['norm_post'][1, 0:1], w['w_o'])
    y, conv1 = _ffn_call(x2.reshape(bsz, steps, D_MODEL), _split3(mods[3]), 1, p, w, conv_state[1])
    kvd = (N_KV_HEADS, HEAD_DIM)
    return (y, rows.reshape(bsz, steps, 4, *kvd), win.reshape(bsz, steps, 2, *kvd),
            s_re[None], s_im[None], jnp.stack([conv0, conv1]))


def kernel(x_prompt, x_sample, cache_nsa_paged, cache_nsa_window, state_ssm_re, state_ssm_im, state_ffn_conv,
           page_table, c_prompt, c_sample, norm_pre, norm_post, w_mod, b_mod, ssm_lam_re, ssm_lam_im,
           ssm_log_dt, ssm_b_re, ssm_b_im, ssm_c_re, ssm_c_im, ssm_d, w_glu, b_glu, ffn_w_up, ffn_conv_w,
           ffn_conv_b, ffn_w_down, norm_kv, w_mod_kv, b_mod_kv, w_kv, cmp_pe, cmp_w1, cmp_b1, cmp_w2,
           w_qg, w_o):
    p = {'norm_pre': norm_pre, 'norm_post': norm_post, 'w_mod': w_mod, 'b_mod': b_mod,
         'ssm_lam_re': ssm_lam_re, 'ssm_lam_im': ssm_lam_im, 'ssm_log_dt': ssm_log_dt,
         'ssm_b_re': ssm_b_re, 'ssm_b_im': ssm_b_im, 'ssm_c_re': ssm_c_re, 'ssm_c_im': ssm_c_im,
         'ssm_d': ssm_d, 'w_glu': w_glu, 'b_glu': b_glu, 'ffn_w_up': ffn_w_up, 'ffn_conv_w': ffn_conv_w,
         'ffn_conv_b': ffn_conv_b, 'ffn_w_down': ffn_w_down, 'norm_kv': norm_kv, 'w_mod_kv': w_mod_kv,
         'b_mod_kv': b_mod_kv, 'w_kv': w_kv, 'cmp_pe': cmp_pe, 'cmp_w1': cmp_w1, 'cmp_b1': cmp_b1,
         'cmp_w2': cmp_w2, 'w_qg': w_qg, 'w_o': w_o}
    w = _prepare(p)
    nb, ns = x_prompt.shape[0], x_sample.shape[0]
    c_all = jnp.concatenate([c_prompt, c_sample])
    mods = _mods(c_all, w_mod.reshape(4, D_MODEL, 3 * D_MODEL), b_mod.reshape(4, 3 * D_MODEL))
    modkv = _mods(c_all, w_mod_kv[None], b_mod_kv[None])[0]
    y_p, rows_p, win_p, sre_p, sim_p, conv_p = _prompt_group(x_prompt, mods[:, :nb], modkv[:nb], p, w)
    y_s, rows_s, win_s, sre_s, sim_s, conv_s = _sample_group(
        x_sample, mods[:, nb:], modkv[nb:], p, w, cache_nsa_paged, page_table, cache_nsa_window,
        state_ssm_re, state_ssm_im, state_ffn_conv)
    return (y_p, y_s, rows_p, rows_s, win_p, win_s, sre_p, sim_p, sre_s, sim_s, conv_p, conv_s)
```

```python
import functools
import math

import numpy as np
import jax
import jax.numpy as jnp
from jax import lax
from jax.experimental import pallas as pl
from jax.experimental.pallas import tpu as pltpu

D_MODEL = 1024
SSM_GROUP = 16
N_GROUPS = D_MODEL // SSM_GROUP
SSM_STATE = 64
N_STATE = N_GROUPS * SSM_STATE
N_HEADS = 16
HEAD_DIM = 64
N_KV_HEADS = 4
GQA_GROUP = N_HEADS // N_KV_HEADS
KV_WIDTH = N_KV_HEADS * HEAD_DIM
CMP_BLOCK = 32
CMP_STRIDE = 16
CMP_HIDDEN = 2 * HEAD_DIM
SEL_BLOCK = 64
N_SEL = 16
WINDOW = 512
ROPE_THETA = 10000.0
D_FF = 2816
CONV_WIDTH = 3
EPS = 1e-6
NEG = -1e30
LOG2E = math.log2(math.e)

SUBLANES = 8
LANES = 128
MXU_DIM = 256
VMEM_LIMIT = 56 * 1024 * 1024

F32 = jnp.float32
BF16 = jnp.bfloat16


def _cparams(*sem):
    return pltpu.CompilerParams(dimension_semantics=sem, vmem_limit_bytes=VMEM_LIMIT)


def _sigmoid(x):
    return 1.0 / (1.0 + jnp.exp(-x))


def _silu(x):
    return x * _sigmoid(x)


def _gelu(x):
    return 0.5 * x * (1.0 + jnp.tanh(math.sqrt(2.0 / math.pi) * (x + 0.044715 * (x * x * x))))


def _rms(x, g):
    return x * lax.rsqrt(jnp.mean(x * x, axis=-1, keepdims=True) + EPS) * g


def _dot(a, b):
    return jnp.dot(a, b, preferred_element_type=F32)


def _dot_nt(a, b):
    return lax.dot_general(a, b, (((1,), (1,)), ((), ())), preferred_element_type=F32)


def _mods_kernel(c_ref, w_ref, b_ref, o_ref):
    c = _silu(c_ref[...])
    o_ref[0] = jnp.dot(c, w_ref[0], preferred_element_type=F32,
                       precision=lax.Precision.HIGHEST) + b_ref[0]


def _mods(c, w, b):
    s, d, n = w.shape
    m = c.shape[0]
    tn = 1024
    return pl.pallas_call(
        _mods_kernel,
        grid=(s, n // tn),
        in_specs=[pl.BlockSpec((m, d), lambda i, j: (0, 0)),
                  pl.BlockSpec((1, d, tn), lambda i, j: (i, 0, j)),
                  pl.BlockSpec((1, 1, tn), lambda i, j: (i, 0, j))],
        out_specs=pl.BlockSpec((1, m, tn), lambda i, j: (i, 0, j)),
        out_shape=jax.ShapeDtypeStruct((s, m, n), F32),
        compiler_params=_cparams("parallel", "parallel"),
        name="adaln_mods",
    )(c, w, b.reshape(s, 1, n))


def _s5_prep_kernel(lr_ref, li_ref, ldt_ref, bre_ref, bim_ref, are_ref, aim_ref, zbre_ref, zbim_ref):
    lr, li = lr_ref[...], li_ref[...]
    dt = jnp.exp(ldt_ref[...])
    mag = jnp.exp(lr * dt)
    a_re, a_im = mag * jnp.cos(li * dt), mag * jnp.sin(li * dt)
    den = lr * lr + li * li
    ir, ii = lr / den, -li / den
    z_re = (a_re - 1.0) * ir - a_im * ii
    z_im = (a_re - 1.0) * ii + a_im * ir
    are_ref[...] = a_re
    aim_ref[...] = a_im
    bre, bim = bre_ref[...], bim_ref[...]
    zr, zi = z_re[:, None, :], z_im[:, None, :]
    zbre_ref[...] = zr * bre - zi * bim
    zbim_ref[...] = zr * bim + zi * bre


def _s5_prep(lam_re, lam_im, log_dt, b_re, b_im):
    g, p, c = b_re.shape
    return pl.pallas_call(
        _s5_prep_kernel,
        out_shape=(jax.ShapeDtypeStruct((g, p), F32), jax.ShapeDtypeStruct((g, p), F32),
                   jax.ShapeDtypeStruct((g, c, p), F32), jax.ShapeDtypeStruct((g, c, p), F32)),
        name="s5_prep",
    )(lam_re, lam_im, log_dt.reshape(g, 1), b_re.transpose(0, 2, 1), b_im.transpose(0, 2, 1))


S5_KBLOCKS = D_MODEL // MXU_DIM
S5_GPB = MXU_DIM // SSM_GROUP
S5_SPB = S5_GPB * SSM_STATE


def _s5_block_diag_in(zb):
    z = zb.reshape(S5_KBLOCKS, S5_GPB, SSM_GROUP, SSM_STATE)
    eye = jnp.eye(S5_GPB, dtype=zb.dtype)
    return jnp.einsum('kgcp,gh->kgchp', z, eye).reshape(S5_KBLOCKS, MXU_DIM, S5_SPB)


def _s5_block_diag_out(cm):
    z = cm.reshape(S5_KBLOCKS, S5_GPB, SSM_GROUP, SSM_STATE)
    eye = jnp.eye(S5_GPB, dtype=cm.dtype)
    return jnp.einsum('kgcp,gh->kgphc', z, eye).reshape(S5_KBLOCKS, S5_SPB, MXU_DIM)


S5_LANE_BLOCKS = 8


def _to_time_major(dst, b, val, t, bt):
    for c in range(val.shape[1] // LANES):
        dst[c, pl.ds(b, t, stride=bt), :] = val[:, c * LANES:(c + 1) * LANES]


def _from_time_major(src, b, t, bt):
    return jnp.concatenate([src[c, pl.ds(b, t, stride=bt), :] for c in range(src.shape[0])], axis=1)


def _slab_cols(src, c0, c1):
    return jnp.concatenate([src[c] for c in range(c0 // LANES, c1 // LANES)], axis=1)


def _s5_kernel(x_ref, mod_ref, npre_ref, npost_ref, wb_ref, wcre_ref, wcim_ref, a_ref, d_ref,
               wglu_ref, bglu_ref, h0_ref, y_ref, sout_ref, htm, bu, ytm, otm, state, *, bt, t):
    nsub = bt // SUBLANES

    @pl.when(pl.program_id(0) == 0)
    def _():
        state[...] = h0_ref[...]

    for b in range(bt):
        hb = _rms(x_ref[b], npre_ref[...]) * (1.0 + mod_ref[1, b:b + 1, :]) + mod_ref[0, b:b + 1, :]
        _to_time_major(htm, b, hb, t, bt)

    for kb in range(S5_KBLOCKS):
        cs = slice(kb * MXU_DIM, (kb + 1) * MXU_DIM)
        hkb = _slab_cols(htm, kb * MXU_DIM, (kb + 1) * MXU_DIM)
        bu[...] = _dot(hkb.astype(BF16), wb_ref[kb])
        for lb0 in range(0, S5_SPB // LANES, S5_LANE_BLOCKS):
            for sb in range(nsub):
                cols = [(lb0 + i) * LANES for i in range(S5_LANE_BLOCKS)]
                rs = slice(sb * SUBLANES, (sb + 1) * SUBLANES)
                ar = [jnp.broadcast_to(a_ref[0:1, kb * S5_SPB + c:kb * S5_SPB + c + LANES], (SUBLANES, LANES))
                      for c in cols]
                ai = [jnp.broadcast_to(a_ref[1:2, kb * S5_SPB + c:kb * S5_SPB + c + LANES], (SUBLANES, LANES))
                      for c in cols]
                init = tuple(state[0, rs, kb * S5_SPB + c:kb * S5_SPB + c + LANES] for c in cols) + \
                    tuple(state[1, rs, kb * S5_SPB + c:kb * S5_SPB + c + LANES] for c in cols)

                def step(i, carry, cols=cols, ar=ar, ai=ai, sb=sb):
                    r0 = pl.multiple_of(i * bt + sb * SUBLANES, SUBLANES)
                    out = [None] * (2 * S5_LANE_BLOCKS)
                    for n, c in enumerate(cols):
                        sr, si = carry[n], carry[S5_LANE_BLOCKS + n]
                        nr = ar[n] * sr - ai[n] * si + bu[pl.ds(r0, SUBLANES), c:c + LANES]
                        ni = ar[n] * si + ai[n] * sr + bu[pl.ds(r0, SUBLANES), S5_SPB + c:S5_SPB + c + LANES]
                        bu[pl.ds(r0, SUBLANES), c:c + LANES] = nr
                        bu[pl.ds(r0, SUBLANES), S5_SPB + c:S5_SPB + c + LANES] = ni
                        out[n], out[S5_LANE_BLOCKS + n] = nr, ni
                    return tuple(out)

                fin = lax.fori_loop(0, t, step, init, unroll=min(t, 8))
                for n, c in enumerate(cols):
                    state[0, rs, kb * S5_SPB + c:kb * S5_SPB + c + LANES] = fin[n]
                    state[1, rs, kb * S5_SPB + c:kb * S5_SPB + c + LANES] = fin[S5_LANE_BLOCKS + n]
        ykb = _dot(bu[:, :S5_SPB].astype(BF16), wcre_ref[kb]) - _dot(bu[:, S5_SPB:].astype(BF16), wcim_ref[kb])
        ytm[:, cs] = ykb + d_ref[:, cs] * hkb

    g = _gelu(ytm[...])
    out = g * _sigmoid(_dot(g.astype(BF16), wglu_ref[...]) + bglu_ref[...])
    out = _rms(out, npost_ref[...])
    for c in range(otm.shape[0]):
        otm[c] = out[:, c * LANES:(c + 1) * LANES]
    for b in range(bt):
        y_ref[b] = x_ref[b] + mod_ref[2, b:b + 1, :] * _from_time_major(otm, b, t, bt)
    sout_ref[...] = state[...]


def _s5_layer(x, mod, npre, npost, wb, wcre, wcim, a, dskip, wglu, bglu, h0, t):
    bsz, seq, d = x.shape
    t = min(t, seq)
    rows = bsz * t
    kern = functools.partial(_s5_kernel, bt=bsz, t=t)
    full = lambda shape: pl.BlockSpec(shape, lambda i: (0,) * len(shape))
    return pl.pallas_call(
        kern,
        grid=(seq // t,),
        in_specs=[pl.BlockSpec((bsz, t, d), lambda i: (0, i, 0)),
                  full((3, bsz, d)), full((1, d)), full((1, d)),
                  full(wb.shape), full(wcre.shape), full(wcim.shape), full((2, N_STATE)), full((1, d)),
                  full((d, d)), full((1, d)), full((2, bsz, N_STATE))],
        out_specs=(pl.BlockSpec((bsz, t, d), lambda i: (0, i, 0)), full((2, bsz, N_STATE))),
        out_shape=(jax.ShapeDtypeStruct((bsz, seq, d), F32), jax.ShapeDtypeStruct((2, bsz, N_STATE), F32)),
        scratch_shapes=[pltpu.VMEM((d // LANES, rows, LANES), F32), pltpu.VMEM((rows, 2 * S5_SPB), F32),
                        pltpu.VMEM((rows, d), F32), pltpu.VMEM((d // LANES, rows, LANES), F32),
                        pltpu.VMEM((2, bsz, N_STATE), F32)],
        compiler_params=_cparams("arbitrary"),
        name="s5_layer",
    )(x, mod, npre, npost, wb, wcre, wcim, a, dskip, wglu, bglu, h0)


def _ffn_kernel(x_ref, mod_ref, npre_ref, npost_ref, wa_ref, wg_ref, cwa_ref, cwg_ref, cba_ref, cbg_ref,
                wd_ref, pfa_ref, pfg_ref, y_ref, sta_ref, stg_ref, htm, hbf, ea, eg, acc, cara, carg, *, bt, t):
    i, j = pl.program_id(0), pl.program_id(1)
    rows = bt * t
    halo = (CONV_WIDTH - 1) * bt

    @pl.when(j == 0)
    def _():
        for b in range(bt):
            hb = _rms(x_ref[b], npre_ref[...]) * (1.0 + mod_ref[1, b:b + 1, :]) + mod_ref[0, b:b + 1, :]
            _to_time_major(htm, b, hb, t, bt)
        hbf[...] = _slab_cols(htm, 0, D_MODEL).astype(BF16)
        acc[...] = jnp.zeros_like(acc)

    @pl.when(i == 0)
    def _():
        cara[j] = pfa_ref[...]
        carg[j] = pfg_ref[...]

    h = hbf[...]

    def conv(e, w_ref, car, cw_ref, cb_ref):
        e[0:halo, :] = car[j]
        e[halo:halo + rows, :] = _dot(h, w_ref[...])
        out = cb_ref[...] + cw_ref[0:1, :] * e[0:rows, :]
        for k in range(1, CONV_WIDTH):
            out = out + cw_ref[k:k + 1, :] * e[k * bt:k * bt + rows, :]
        car[j] = e[rows:rows + halo, :]
        return out

    a = conv(ea, wa_ref, cara, cwa_ref, cba_ref)
    g = conv(eg, wg_ref, carg, cwg_ref, cbg_ref)
    acc[...] += _dot((_silu(g) * a).astype(BF16), wd_ref[...])
    sta_ref[j] = cara[j]
    stg_ref[j] = carg[j]

    @pl.when(j == pl.num_programs(1) - 1)
    def _():
        out = _rms(acc[...], npost_ref[...])
        for c in range(htm.shape[0]):
            htm[c] = out[:, c * LANES:(c + 1) * LANES]
        for b in range(bt):
            y_ref[b] = x_ref[b] + mod_ref[2, b:b + 1, :] * _from_time_major(htm, b, t, bt)


def _ffn_layer(x, mod, npre, npost, wa, wg, cwa, cwg, cba, cbg, wd, pfa, pfg, t, tn):
    bsz, seq, d = x.shape
    t = min(t, seq)
    rows = bsz * t
    halo = (CONV_WIDTH - 1) * bsz
    nj = D_FF // tn
    kern = functools.partial(_ffn_kernel, bt=bsz, t=t)
    full = lambda shape: pl.BlockSpec(shape, lambda i, j: (0,) * len(shape))
    col = lambda r: pl.BlockSpec((r, tn), lambda i, j: (0, j))
    return pl.pallas_call(
        kern,
        grid=(seq // t, nj),
        in_specs=[pl.BlockSpec((bsz, t, d), lambda i, j: (0, i, 0)),
                  full((3, bsz, d)), full((1, d)), full((1, d)),
                  col(d), col(d), col(CONV_WIDTH), col(CONV_WIDTH), col(1), col(1),
                  pl.BlockSpec((tn, d), lambda i, j: (j, 0)), col(halo), col(halo)],
        out_specs=(pl.BlockSpec((bsz, t, d), lambda i, j: (0, i, 0)), full((nj, halo, tn)), full((nj, halo, tn))),
        out_shape=(jax.ShapeDtypeStruct((bsz, seq, d), F32), jax.ShapeDtypeStruct((nj, halo, tn), F32),
                   jax.ShapeDtypeStruct((nj, halo, tn), F32)),
        scratch_shapes=[pltpu.VMEM((d // LANES, rows, LANES), F32), pltpu.VMEM((rows, d), BF16),
                        pltpu.VMEM((rows + halo, tn), F32),
                        pltpu.VMEM((rows + halo, tn), F32), pltpu.VMEM((rows, d), F32),
                        pltpu.VMEM((nj, halo, tn), F32), pltpu.VMEM((nj, halo, tn), F32)],
        compiler_params=_cparams("arbitrary", "arbitrary"),
        name="conv_ffn",
    )(x, mod, npre, npost, wa, wg, cwa, cwg, cba, cbg, wd, pfa, pfg)


def _rope_tables(pos):
    half = HEAD_DIM // 2
    freqs = ROPE_THETA ** (-np.arange(half, dtype=np.float64) / half)
    ang = pos.astype(np.float64)[:, None] * freqs[None, :]
    cos = np.concatenate([np.cos(ang), np.cos(ang)], axis=1)
    sin = np.concatenate([-np.sin(ang), np.sin(ang)], axis=1)
    rep = LANES // HEAD_DIM
    return (jnp.asarray(np.tile(cos, (1, rep)), F32), jnp.asarray(np.tile(sin, (1, rep)), F32))


def _rope(x, cos, sin):
    half = HEAD_DIM // 2
    first = (lax.broadcasted_iota(jnp.int32, cos.shape, 1) % HEAD_DIM) < half
    out = []
    for c in range(x.shape[1] // LANES):
        s = x[:, c * LANES:(c + 1) * LANES]
        swapped = jnp.where(first, pltpu.roll(s, LANES - half, 1), pltpu.roll(s, half, 1))
        out.append(s * cos + swapped * sin)
    return jnp.concatenate(out, axis=1)


QG_PAD = N_HEADS * HEAD_DIM + LANES
NSA_TQ = 256
NSA_TK = 256
NSA_ROWS = 128
NSA_COLS = 256


def _proj_kernel(x_ref, mkv_ref, mq_ref, nkv_ref, nq_ref, wkv_ref, wqg_ref, cos_ref, sin_ref,
                 rows_ref, win_ref, kvb_ref, qp_ref, qr_ref, gate_ref, *rest, bt, t, transposed):
    if transposed:
        vt_ref, hkv, hq = rest
    else:
        hkv, hq = rest
    for b in range(bt):
        xb = x_ref[b]
        xn = xb * lax.rsqrt(jnp.mean(xb * xb, axis=-1, keepdims=True) + EPS)
        hkv[b * t:(b + 1) * t, :] = (xn * nkv_ref[...]) * (1.0 + mkv_ref[1, b]) + mkv_ref[0, b]
        hq[b * t:(b + 1) * t, :] = (xn * nq_ref[...]) * (1.0 + mq_ref[1, b]) + mq_ref[0, b]
    cos, sin = cos_ref[...], sin_ref[...]
    kv = _dot(hkv[...].astype(BF16), wkv_ref[...])
    w = KV_WIDTH
    ksel = _rope(kv[:, 2 * w:3 * w], cos, sin)
    kwin = _rope(kv[:, 4 * w:5 * w], cos, sin)
    rows = jnp.concatenate([kv[:, :2 * w], ksel, kv[:, 3 * w:4 * w]], axis=1)
    win = jnp.concatenate([kwin, kv[:, 5 * w:]], axis=1)
    qg = _dot(hq[...].astype(BF16), wqg_ref[...])
    q = qg[:, :N_HEADS * HEAD_DIM] * (HEAD_DIM ** -0.5)
    qrot = _rope(q, cos, sin)
    gates = _sigmoid(qg[:, N_HEADS * HEAD_DIM:])
    if not transposed:
        for b in range(bt):
            rs = slice(b * t, (b + 1) * t)
            rows_ref[b] = rows[rs]
            win_ref[b] = win[rs]
            kvb_ref[b] = jnp.concatenate([rows[rs, 2 * w:], win[rs]], axis=1).astype(BF16)
            qp_ref[b] = q[rs].astype(BF16)
            qr_ref[b] = qrot[rs].astype(BF16)
            gate_ref[b] = gates[rs]
        return
    rows_ref[0] = rows
    win_ref[0] = win
    kvb_ref[0] = jnp.concatenate([ksel, kwin], axis=1).astype(BF16)
    qp_ref[0] = (q * LOG2E).T.astype(BF16)
    qr_ref[0] = (qrot * LOG2E).T.astype(BF16)
    gate_ref[0] = gates.T
    ones = jnp.ones((HEAD_DIM, NSA_TK), F32)
    for j in range(t // NSA_TK):
        rs = slice(j * NSA_TK, (j + 1) * NSA_TK)
        vt = jnp.concatenate([kv[rs, 3 * w:4 * w], kv[rs, 5 * w:]], axis=1).T
        parts = []
        for h in range(2 * N_KV_HEADS):
            parts += [vt[h * HEAD_DIM:(h + 1) * HEAD_DIM], ones]
        vt_ref[0, j] = jnp.concatenate(parts, axis=0).astype(BF16)


def _proj(x, mkv, mq, nkv, nq, wkv, wqg, cos, sin, bt, t, transposed):
    bsz, seq, d = x.shape
    t = min(t, seq)
    r = bt * t
    kern = functools.partial(_proj_kernel, bt=bt, t=t, transposed=transposed)
    full = lambda shape: pl.BlockSpec(shape, lambda b, i: (0,) * len(shape))
    tile = lambda n: pl.BlockSpec((bt, t, n), lambda b, i: (b, i, 0))
    tile_t = lambda n: pl.BlockSpec((bt, n, t), lambda b, i: (b, 0, i))
    mod = lambda n: pl.BlockSpec((n, bt, 1, d), lambda b, i: (0, b, 0, 0))
    tab = pl.BlockSpec((r, LANES), lambda b, i: (i, 0))
    nq_cols = N_HEADS * HEAD_DIM
    sds = jax.ShapeDtypeStruct
    wlen = min(WINDOW, seq)
    assert wlen % t == 0
    first_kept = (seq - wlen) // t
    win_tile = pl.BlockSpec((bt, t, 2 * KV_WIDTH), lambda b, i: (b, jnp.maximum(i - first_kept, 0), 0))
    out_specs = [tile(4 * KV_WIDTH), win_tile]
    out_shape = [sds((bsz, seq, 4 * KV_WIDTH), F32), sds((bsz, wlen, 2 * KV_WIDTH), F32)]
    if transposed:
        assert bt == 1 and t % NSA_TK == 0
        vrows = 2 * N_KV_HEADS * LANES
        out_specs += [tile(2 * KV_WIDTH), tile_t(nq_cols), tile_t(nq_cols), tile_t(LANES),
                      pl.BlockSpec((1, t // NSA_TK, vrows, NSA_TK), lambda b, i: (b, i, 0, 0))]
        out_shape += [sds((bsz, seq, 2 * KV_WIDTH), BF16), sds((bsz, nq_cols, seq), BF16),
                      sds((bsz, nq_cols, seq), BF16), sds((bsz, LANES, seq), F32),
                      sds((bsz, seq // NSA_TK, vrows, NSA_TK), BF16)]
    else:
        out_specs += [tile(4 * KV_WIDTH), tile(nq_cols), tile(nq_cols), tile(LANES)]
        out_shape += [sds((bsz, seq, 4 * KV_WIDTH), BF16), sds((bsz, seq, nq_cols), BF16),
                      sds((bsz, seq, nq_cols), BF16), sds((bsz, seq, LANES), F32)]
    return pl.pallas_call(
        kern,
        grid=(bsz // bt, seq // t),
        in_specs=[tile(d), mod(2), mod(3), full((1, d)), full((1, d)), full(wkv.shape), full(wqg.shape),
                  tab, tab],
        out_specs=tuple(out_specs),
        out_shape=tuple(out_shape),
        scratch_shapes=[pltpu.VMEM((r, d), F32), pltpu.VMEM((r, d), F32)],
        compiler_params=_cparams("parallel", "arbitrary"),
        name="kv_q_proj",
    )(x, mkv, mq, nkv, nq, wkv, wqg, cos, sin)


PAGE_SIZE = 128
CHUNKS_PER_PAGE = PAGE_SIZE // CMP_STRIDE
HEADS_PER_SLAB = LANES // HEAD_DIM
SLABS_PER_SLOT = KV_WIDTH // LANES


def _compress_kernel(tbl_ref, *refs, n_pages, transposed):
    page_refs = refs[:n_pages]
    pe_ref, w1_ref, w1bd_ref, b1_ref, w2_ref, perm_ref, out_ref, outt_ref, zs, xs = refs[n_pages:]
    m = n_pages * CHUNKS_PER_PAGE
    pb = jnp.dot(pe_ref[0], w1_ref[0], preferred_element_type=F32, precision=lax.Precision.HIGHEST)
    bias = pb[0:1, :CMP_HIDDEN] + pb[1:2, CMP_HIDDEN:] + b1_ref[0]
    for sl in range(SLABS_PER_SLOT):
        hd = slice(sl * LANES, (sl + 1) * LANES)
        for k, pr in enumerate(page_refs):
            zs[k * LANES:(k + 1) * LANES, :] = (pr[0, hd, :] if transposed else pr[0, :, hd].T).astype(BF16)
        zp = _dot(zs[...], perm_ref[...])
        for k in range(n_pages):
            xs[k] = zp[k * LANES:(k + 1) * LANES, :].T
        flat = jnp.concatenate([xs[:, l * CHUNKS_PER_PAGE:(l + 1) * CHUNKS_PER_PAGE, :].reshape(m, LANES)
                                for l in range(CMP_STRIDE)], axis=1)
        pab2 = _dot(flat.astype(BF16), w1bd_ref[0])
        hid = []
        for h in range(HEADS_PER_SLAB):
            pab = pab2[:, h * 2 * CMP_HIDDEN:(h + 1) * 2 * CMP_HIDDEN]
            nxt = pltpu.roll(pab[:, CMP_HIDDEN:], m - 1, 0)
            hid.append(_gelu(pab[:, :CMP_HIDDEN] + nxt + bias))
        out = _dot(jnp.concatenate(hid, axis=1).astype(BF16), w2_ref[0])
        out_ref[0, :, hd] = out.astype(out_ref.dtype)
        outt_ref[0, hd, :] = out.T.astype(outt_ref.dtype)


def _compress(src, table, n_pages, pe2, w1f, w1bd, b1, w2bd, transposed):
    bsz = table.shape[0] // n_pages
    m = n_pages * CHUNKS_PER_PAGE
    kern = functools.partial(_compress_kernel, n_pages=n_pages, transposed=transposed)
    if transposed:
        page = lambda k: pl.BlockSpec((1, KV_WIDTH, PAGE_SIZE), lambda b, s, tbl: (tbl[b * n_pages + k], s, 0))
    else:
        page = lambda k: pl.BlockSpec((1, PAGE_SIZE, KV_WIDTH), lambda b, s, tbl: (tbl[b * n_pages + k], 0, s))
    wspec = lambda shape: pl.BlockSpec((1,) + shape, lambda b, s, tbl: (s, 0, 0))
    r = np.arange(PAGE_SIZE)
    perm = jnp.asarray((r % CMP_STRIDE * CHUNKS_PER_PAGE + r // CMP_STRIDE)[:, None] == r[None, :], BF16)
    return pl.pallas_call(
        kern,
        grid_spec=pltpu.PrefetchScalarGridSpec(
            num_scalar_prefetch=1,
            grid=(bsz, 2),
            in_specs=[page(k) for k in range(n_pages)] + [
                wspec(pe2.shape[1:]), wspec(w1f.shape[1:]), wspec(w1bd.shape[1:]), wspec(b1.shape[1:]),
                wspec(w2bd.shape[1:]), pl.BlockSpec(perm.shape, lambda b, s, tbl: (0, 0))],
            out_specs=(pl.BlockSpec((1, m, KV_WIDTH), lambda b, s, tbl: (b, 0, s)),
                       pl.BlockSpec((1, KV_WIDTH, m), lambda b, s, tbl: (b, s, 0))),
            scratch_shapes=[pltpu.VMEM((n_pages * LANES, PAGE_SIZE), BF16),
                            pltpu.VMEM((n_pages, PAGE_SIZE, LANES), F32)],
        ),
        out_shape=(jax.ShapeDtypeStruct((bsz, m, 2 * KV_WIDTH), BF16),
                   jax.ShapeDtypeStruct((bsz, 2 * KV_WIDTH, m), BF16)),
        compiler_params=_cparams("parallel", "parallel"),
        name="compress_%d" % n_pages,
    )(table, *([src] * n_pages), pe2, w1f, w1bd, b1, w2bd, perm)


def _sel_matrix(nc, ns, rows, cols):
    r, q = SEL_BLOCK // CMP_STRIDE, CMP_BLOCK // CMP_STRIDE
    offs = (np.arange(r)[:, None] - np.arange(q)[None, :]).reshape(-1)
    target = r * np.arange(ns)[None, :, None] + offs[None, None, :]
    m = (np.arange(nc)[:, None, None] == target).sum(-1).astype(np.float32)
    out = np.zeros((rows, cols), np.float32)
    out[:nc, :ns] = m
    return out


def _select_mask_t(score_t, qpos, ns, n_sel):
    j = lax.broadcasted_iota(jnp.int32, score_t.shape, 0)
    cur = qpos // SEL_BLOCK
    forced = (j == 0) | (j == cur) | (j == cur - 1)
    valid = j * SEL_BLOCK <= qpos
    s = jnp.where(forced, jnp.inf, jnp.where(valid, score_t, -jnp.inf))
    s = jnp.where(j < ns, s, -jnp.inf)
    rank = jnp.zeros(score_t.shape, jnp.int32)
    for jp in range(ns):
        sp = s[jp:jp + 1, :]
        ahead = (sp > s) | ((sp == s) & (jp < j))
        rank = rank + ahead.astype(jnp.int32)
    return ((rank < n_sel) & (j < ns)).astype(F32)


def _nsa_prompt_kernel(x_ref, qpt_ref, qrt_ref, gatet_ref, cmp_ref, cmpt_ref, kk_ref, vt_ref, mod_ref, npost_ref,
                       wot_ref, mselt_ref, expandt_ref, y_ref, bias_sel, bias_win, q4t_s, m_s, a_s, acc_s, s_s, p_s,
                       ot_s, *, tq, tk, seq, ns):
    qt = pl.program_id(1)
    t0 = qt * tq
    w = KV_WIDTH
    grp = GQA_GROUP
    nq = grp * tq
    nc_pad = cmp_ref.shape[1]
    qpos_row = t0 + lax.broadcasted_iota(jnp.int32, (1, tq), 1)
    cends = lax.broadcasted_iota(jnp.int32, (nc_pad, 1), 0) * CMP_STRIDE + (CMP_BLOCK - 1)
    cvalid = cends <= qpos_row
    cbias = jnp.concatenate([jnp.where(cvalid, 0.0, NEG)] * grp, axis=1)
    cmask = jnp.concatenate([cvalid.astype(F32)] * grp, axis=1)
    n_sel_tiles = (t0 + tq + tk - 1) // tk
    n_win_tiles = (WINDOW + tq + tk - 1) // tk

    for i in range(n_win_tiles):
        kpos = t0 + tq - (i + 1) * tk + lax.broadcasted_iota(jnp.int32, (tk, 1), 0)
        diff = qpos_row - kpos
        bias_win[i] = jnp.where((diff >= 0) & (diff < WINDOW) & (kpos >= 0), 0.0, NEG)
    kpos_diag = t0 + lax.broadcasted_iota(jnp.int32, (tq, 1), 0)

    def attend(kcol, vrow, n_tiles, unroll, rem_step, start_of, bias_of):
        m_s[...] = jnp.full(m_s.shape, NEG, F32)
        acc_s[...] = jnp.zeros(acc_s.shape, F32)

        def tile(i, u):
            k0 = pl.multiple_of(start_of(i), tk)
            k = kk_ref[0, pl.ds(k0, tk), kcol:kcol + HEAD_DIM]
            vt = vt_ref[0, k0 // tk, vrow:vrow + LANES, :]
            s_s[u] = _dot(k, q4t_s[...])
            for c in range(nq // NSA_COLS):
                cols = slice(c * NSA_COLS, (c + 1) * NSA_COLS)
                top = None
                for r in range(tk // NSA_ROWS):
                    rows = slice(r * NSA_ROWS, (r + 1) * NSA_ROWS)
                    s = s_s[u, rows, cols] + bias_of(i, k0, r * NSA_ROWS, (c * NSA_COLS) % tq)
                    s_s[u, rows, cols] = s
                    cur = jnp.max(s, axis=0, keepdims=True)
                    top = cur if top is None else jnp.maximum(top, cur)
                m_old = m_s[:, cols]
                m_new = jnp.maximum(m_old, top)
                m_rep = jnp.concatenate([m_new] * (NSA_ROWS // SUBLANES), axis=0)
                for r in range(tk // NSA_ROWS):
                    rows = slice(r * NSA_ROWS, (r + 1) * NSA_ROWS)
                    p_s[u, rows, cols] = jnp.exp2(s_s[u, rows, cols] - m_rep).astype(BF16)
                a_s[u, :, cols] = jnp.exp2(m_old - m_new)
                m_s[:, cols] = m_new
            alpha = jnp.concatenate([a_s[u]] * (LANES // SUBLANES), axis=0)
            acc_s[...] = alpha * acc_s[...] + _dot(vt, p_s[u])

        def body(j, carry):
            for u in range(unroll):
                tile(j * unroll + u, u)
            return carry

        n_full = n_tiles // unroll
        lax.fori_loop(0, n_full, body, 0)
        if rem_step:
            @pl.when(n_tiles - n_full * unroll >= rem_step)
            def _():
                for u in range(rem_step):
                    tile(n_full * unroll + u, u)
        return acc_s[:HEAD_DIM, :] / acc_s[HEAD_DIM:, :]

    for kh in range(N_KV_HEADS):
        heads = [kh * grp + g for g in range(grp)]
        for g, h in enumerate(heads):
            q4t_s[:, g * tq:(g + 1) * tq] = qpt_ref[0, h * HEAD_DIM:(h + 1) * HEAD_DIM, :]
        s = _dot(cmp_ref[0, :, kh * HEAD_DIM:(kh + 1) * HEAD_DIM], q4t_s[...]) + cbias
        e = jnp.exp2(s - jnp.max(s, axis=0, keepdims=True))
        p = e / jnp.sum(e, axis=0, keepdims=True) * cmask
        o_cmp = _dot(cmpt_ref[0, w + kh * HEAD_DIM:w + (kh + 1) * HEAD_DIM, :], p.astype(BF16))
        psum = p[:, 0:tq]
        for g in range(1, grp):
            psum = psum + p[:, g * tq:(g + 1) * tq]
        score_t = jnp.dot(mselt_ref[...], psum, preferred_element_type=F32, precision=lax.Precision.HIGHEST)
        sel_t = _select_mask_t(score_t, qpos_row, ns, N_SEL)
        bias_sel[...] = _dot(expandt_ref[...], jnp.where(sel_t > 0.5, 0.0, NEG).astype(BF16))
        diag = pl.ds(pl.multiple_of(t0, tq), tq)
        bias_sel[diag, :] = jnp.where(kpos_diag <= qpos_row, bias_sel[diag, :], NEG)

        for g, h in enumerate(heads):
            q4t_s[:, g * tq:(g + 1) * tq] = qrt_ref[0, h * HEAD_DIM:(h + 1) * HEAD_DIM, :]
        o_sel = attend(kh * HEAD_DIM, kh * LANES, n_sel_tiles, 2 * (tq // tk), tq // tk, lambda i: i * tk,
                       lambda i, k0, r0, c0: bias_sel[pl.ds(k0 + r0, NSA_ROWS), c0:c0 + NSA_COLS])
        o_win = attend(w + kh * HEAD_DIM, (N_KV_HEADS + kh) * LANES, n_win_tiles, n_win_tiles, 0,
                       lambda i: jnp.maximum(t0 + tq - (i + 1) * tk, 0),
                       lambda i, k0, r0, c0: bias_win[i, r0:r0 + NSA_ROWS, c0:c0 + NSA_COLS])
        for g, h in enumerate(heads):
            cs = slice(g * tq, (g + 1) * tq)
            gc = gatet_ref[0, h:h + 1, :]
            gs = gatet_ref[0, N_HEADS + h:N_HEADS + h + 1, :]
            gw = gatet_ref[0, 2 * N_HEADS + h:2 * N_HEADS + h + 1, :]
            ot_s[h * HEAD_DIM:(h + 1) * HEAD_DIM, :] = (
                gc * o_cmp[:, cs] + gs * o_sel[:, cs] + gw * o_win[:, cs]).astype(BF16)

    out = _dot(wot_ref[...], ot_s[...]).T
    y_ref[0] = x_ref[0] + mod_ref[0] * _rms(out, npost_ref[...])


def _nsa_prompt(x, qpt, qrt, gatet, cmp, cmpt, kk, vt, gate_mod, npost, wot, tq, tk):
    bsz, seq, d = x.shape
    tq = min(tq, seq)
    assert tq % tk == 0 and seq % tq == 0 and tq % NSA_COLS == 0 and vt.shape[3] == tk
    nc = (seq - CMP_BLOCK) // CMP_STRIDE + 1
    ns = -(-seq // SEL_BLOCK)
    nc_pad = cmp.shape[1]
    j_pad = -(-ns // 16) * 16
    n_win = (WINDOW + tq + tk - 1) // tk
    mselt = jnp.asarray(_sel_matrix(nc, ns, nc_pad, j_pad).T)
    expandt = jnp.asarray((np.arange(seq)[:, None] // SEL_BLOCK == np.arange(j_pad)[None, :]), BF16)
    kern = functools.partial(_nsa_prompt_kernel, tq=tq, tk=tk, seq=seq, ns=ns)
    full = lambda shape: pl.BlockSpec(shape, lambda b, i: (0,) * len(shape))
    tile = lambda n: pl.BlockSpec((1, tq, n), lambda b, i: (b, i, 0))
    tile_t = lambda n: pl.BlockSpec((1, n, tq), lambda b, i: (b, 0, i))
    per_b = lambda *shape: pl.BlockSpec((1,) + shape, lambda b, i: (b,) + (0,) * len(shape))
    nq = GQA_GROUP * tq
    return pl.pallas_call(
        kern,
        grid=(bsz, seq // tq),
        in_specs=[tile(d), tile_t(qpt.shape[1]), tile_t(qrt.shape[1]), tile_t(LANES), per_b(*cmp.shape[1:]),
                  per_b(*cmpt.shape[1:]), per_b(*kk.shape[1:]), per_b(*vt.shape[1:]), per_b(1, d), full((1, d)),
                  full(wot.shape), full(mselt.shape), full(expandt.shape)],
        out_specs=tile(d),
        out_shape=jax.ShapeDtypeStruct((bsz, seq, d), F32),
        scratch_shapes=[pltpu.VMEM((seq, tq), F32), pltpu.VMEM((n_win, tk, tq), F32),
                        pltpu.VMEM((HEAD_DIM, nq), BF16), pltpu.VMEM((SUBLANES, nq), F32),
                        pltpu.VMEM((n_win, SUBLANES, nq), F32), pltpu.VMEM((LANES, nq), F32),
                        pltpu.VMEM((n_win, tk, nq), F32), pltpu.VMEM((n_win, tk, nq), BF16),
                        pltpu.VMEM((d, tq), BF16)],
        compiler_params=_cparams("parallel", "arbitrary"),
        name="nsa_prompt",
    )(x, qpt, qrt, gatet, cmp, cmpt, kk, vt, gate_mod, npost, wot, mselt, expandt)


def _nsa_sample_kernel(tbl_ref, *refs, n_pages, pages_per_step, past_len, n_steps, ns):
    page_refs = refs[:pages_per_step]
    (qp_ref, qr_ref, gate_ref, cmp_ref, new_ref, wc_ref, mselt_ref, expand_ref, expand_new_ref, out_ref,
     selt, m_s, l_s, acc_s, ocmp_s) = refs[pages_per_step:]
    pc = pl.program_id(1)
    w = KV_WIDTH
    nrow = qp_ref.shape[1]
    step_of_row = lax.broadcasted_iota(jnp.int32, (nrow, 1), 0) % n_steps
    qpos_col = past_len + step_of_row
    qr = qr_ref[0]

    @pl.when(pc == 0)
    def _():
        nc_pad = cmp_ref.shape[1]
        cends = lax.broadcasted_iota(jnp.int32, (1, nc_pad), 1) * CMP_STRIDE + (CMP_BLOCK - 1)
        cvalid = cends <= qpos_col
        s = jnp.where(cvalid, _dot_nt(qp_ref[0], cmp_ref[0, :, :w]), NEG)
        e = jnp.exp(s - jnp.max(s, axis=-1, keepdims=True))
        p = jnp.where(cvalid, e / jnp.sum(e, axis=-1, keepdims=True), 0.0)
        ocmp_s[...] = _dot(p.astype(BF16), cmp_ref[0, :, w:])
        grp = nrow // GQA_GROUP
        psum = p[0:grp]
        for g in range(1, GQA_GROUP):
            psum = psum + p[g * grp:(g + 1) * grp]
        psum = jnp.concatenate([psum] * GQA_GROUP, axis=0)
        score_t = lax.dot_general(mselt_ref[...], psum, (((1,), (1,)), ((), ())),
                                  preferred_element_type=F32, precision=lax.Precision.HIGHEST)
        qpos_row = past_len + lax.broadcasted_iota(jnp.int32, (1, nrow), 1) % n_steps
        selt[...] = _select_mask_t(score_t, qpos_row, ns, N_SEL).astype(BF16)
        m_s[...] = jnp.full(m_s.shape, NEG, F32)
        l_s[...] = jnp.zeros(l_s.shape, F32)
        acc_s[...] = jnp.zeros(acc_s.shape, F32)

    def flash(s, allowed, pv):
        s = jnp.where(allowed, s, NEG)
        m_old = m_s[...]
        m_new = jnp.maximum(m_old, jnp.max(s, axis=-1, keepdims=True))
        alpha = jnp.exp(m_old - m_new)
        p = jnp.exp(s - m_new)
        l_s[...] = alpha * l_s[...] + jnp.sum(p, axis=-1, keepdims=True)
        acc_s[...] = alpha * acc_s[...] + pv(p.astype(BF16))
        m_s[...] = m_new

    kt = jnp.concatenate([pr[0, :w, :] for pr in page_refs], axis=1).astype(BF16)
    vt = jnp.concatenate([pr[0, w:, :] for pr in page_refs], axis=1).astype(BF16)
    allowed = lax.dot_general(selt[...], expand_ref[...], (((0,), (0,)), ((), ())),
                              preferred_element_type=F32) > 0.5
    flash(_dot(qr, kt), allowed, lambda p: _dot_nt(p, vt))

    @pl.when(pc == pl.num_programs(1) - 1)
    def _():
        pad = jnp.zeros((LANES - n_steps, 4 * w), F32)
        new = jnp.concatenate([new_ref[0].astype(F32), pad], axis=0).astype(BF16)
        key_step = lax.broadcasted_iota(jnp.int32, (1, LANES), 1)
        causal_new = key_step <= step_of_row
        picked = lax.dot_general(selt[...], expand_new_ref[...], (((0,), (0,)), ((), ())),
                                 preferred_element_type=F32) > 0.5
        flash(_dot_nt(qr, new[:, :w]), causal_new & picked, lambda p: _dot(p, new[:, w:2 * w]))
        o_sel = acc_s[...] / l_s[...]

        w_buf = wc_ref.shape[2]
        buf_idx = lax.broadcasted_iota(jnp.int32, (1, w_buf), 1)
        diff_buf = step_of_row + w_buf - buf_idx
        s_buf = jnp.where(diff_buf < WINDOW, _dot(qr, wc_ref[0, :w, :].astype(BF16)), NEG)
        s_new = jnp.where(causal_new, _dot_nt(qr, new[:, 2 * w:3 * w]), NEG)
        sw = jnp.concatenate([s_buf, s_new], axis=1)
        ew = jnp.exp(sw - jnp.max(sw, axis=-1, keepdims=True))
        pw = (ew / jnp.sum(ew, axis=-1, keepdims=True)).astype(BF16)
        o_win = _dot_nt(pw[:, :w_buf], wc_ref[0, w:, :].astype(BF16)) + _dot(pw[:, w_buf:], new[:, 3 * w:])
        g = gate_ref[0]
        out_ref[0] = g[:, 0:1] * ocmp_s[...] + g[:, 1:2] * o_sel + g[:, 2:3] * o_win


SAMPLE_PAGES_PER_STEP = 16


def _nsa_sample(pool, table, n_pages, qp_bd, qr_bd, gates_r, cmp, new_kvb, wcache, past_len):
    bsz, nrow, _ = qp_bd.shape
    n_steps = new_kvb.shape[1]
    pps = min(SAMPLE_PAGES_PER_STEP, n_pages)
    nc = (past_len + n_steps - CMP_BLOCK) // CMP_STRIDE + 1
    ns = -(-(past_len + n_steps) // SEL_BLOCK)
    assert ns == past_len // SEL_BLOCK + 1 and n_steps <= SEL_BLOCK
    nc_pad = cmp.shape[1]
    j_pad = -(-ns // 16) * 16
    mselt = jnp.asarray(_sel_matrix(nc, ns, nc_pad, j_pad).T)
    expand = jnp.asarray((np.arange(j_pad)[:, None] == np.arange(past_len)[None, :] // SEL_BLOCK), BF16)
    expand_new = jnp.asarray(np.broadcast_to(np.arange(j_pad)[:, None] == ns - 1, (j_pad, LANES)), BF16)
    kern = functools.partial(_nsa_sample_kernel, n_pages=n_pages, pages_per_step=pps, past_len=past_len,
                             n_steps=n_steps, ns=ns)
    sel_rows = 1
    page = lambda k: pl.BlockSpec((1, 2 * KV_WIDTH, PAGE_SIZE),
                                  lambda b, c, tbl: (tbl[b * n_pages + c * pps + k], sel_rows, 0))
    per_b = lambda r, n: pl.BlockSpec((1, r, n), lambda b, c, tbl: (b, 0, 0))
    full = lambda shape: pl.BlockSpec(shape, lambda b, c, tbl: (0,) * len(shape))
    return pl.pallas_call(
        kern,
        grid_spec=pltpu.PrefetchScalarGridSpec(
            num_scalar_prefetch=1,
            grid=(bsz, n_pages // pps),
            in_specs=[page(k) for k in range(pps)] + [
                per_b(nrow, KV_WIDTH), per_b(nrow, KV_WIDTH), per_b(nrow, LANES), per_b(nc_pad, 2 * KV_WIDTH),
                per_b(n_steps, 4 * KV_WIDTH), per_b(2 * KV_WIDTH, wcache.shape[2]), full(mselt.shape),
                pl.BlockSpec((j_pad, pps * PAGE_SIZE), lambda b, c, tbl: (0, c)), full(expand_new.shape)],
            out_specs=per_b(nrow, KV_WIDTH),
            scratch_shapes=[pltpu.VMEM((j_pad, nrow), BF16), pltpu.VMEM((nrow, 1), F32),
                            pltpu.VMEM((nrow, 1), F32), pltpu.VMEM((nrow, KV_WIDTH), F32),
                            pltpu.VMEM((nrow, KV_WIDTH), F32)],
        ),
        out_shape=jax.ShapeDtypeStruct((bsz, nrow, KV_WIDTH), F32),
        compiler_params=_cparams("parallel", "arbitrary"),
        name="nsa_sample",
    )(table, *([pool] * pps), qp_bd, qr_bd, gates_r, cmp, new_kvb, wcache, mselt, expand, expand_new)


def _oproj_kernel(o_ref, x_ref, gate_ref, npost_ref, wo_ref, y_ref):
    out = _dot(o_ref[...].astype(BF16), wo_ref[...])
    y_ref[...] = x_ref[...] + gate_ref[...] * _rms(out, npost_ref[...])


def _oproj(o, x, gate, npost, wo):
    return pl.pallas_call(
        _oproj_kernel,
        out_shape=jax.ShapeDtypeStruct(x.shape, F32),
        compiler_params=pltpu.CompilerParams(vmem_limit_bytes=VMEM_LIMIT),
        name="nsa_out_proj",
    )(o, x, gate, npost, wo)


S5_T = 64
FFN_T = 64
FFN_TN = D_FF


def _prepare(p):
    w = {}
    a_re, a_im, zb_re, zb_im = _s5_prep(p['ssm_lam_re'][0], p['ssm_lam_im'][0], p['ssm_log_dt'][0],
                                        p['ssm_b_re'][0], p['ssm_b_im'][0])
    w['s5_a'] = jnp.stack([a_re.reshape(-1), a_im.reshape(-1)])
    w['s5_wb'] = jnp.concatenate([_s5_block_diag_in(zb_re), _s5_block_diag_in(zb_im)], axis=-1).astype(BF16)
    w['s5_wcre'] = _s5_block_diag_out(p['ssm_c_re'][0]).astype(BF16)
    w['s5_wcim'] = _s5_block_diag_out(p['ssm_c_im'][0]).astype(BF16)
    w['w_glu'] = p['w_glu'][0].astype(BF16)
    w['ffn_wa'] = p['ffn_w_up'][:, :, :D_FF].astype(BF16)
    w['ffn_wg'] = p['ffn_w_up'][:, :, D_FF:].astype(BF16)
    w['ffn_wd'] = p['ffn_w_down'].astype(BF16)
    w['w_kv'] = p['w_kv'].astype(BF16)
    w['w_qg'] = jnp.pad(p['w_qg'][0], ((0, 0), (0, QG_PAD - p['w_qg'].shape[2]))).astype(BF16)
    w['w_o'] = p['w_o'][0].astype(BF16)
    w['w_o_t'] = p['w_o'][0].T.astype(BF16)
    half = CMP_BLOCK // 2
    w1r = p['cmp_w1'].reshape(2, CMP_BLOCK, HEAD_DIM, CMP_HIDDEN)
    w['cmp_w1'] = jnp.concatenate([w1r[:, :half], w1r[:, half:]], axis=-1).reshape(
        2, half * HEAD_DIM, 2 * CMP_HIDDEN)
    w['cmp_pe'] = p['cmp_pe'].reshape(2, 2, half * HEAD_DIM)
    w['cmp_b1'] = p['cmp_b1'].reshape(2, 1, CMP_HIDDEN)
    eye = jnp.eye(HEADS_PER_SLAB, dtype=F32)
    w['cmp_w1bd'] = jnp.einsum('sldn,ab->sladbn', w['cmp_w1'].reshape(2, half, HEAD_DIM, 2 * CMP_HIDDEN),
                               eye).reshape(2, half * LANES, HEADS_PER_SLAB * 2 * CMP_HIDDEN).astype(BF16)
    w['cmp_w2'] = jnp.einsum('shd,ab->sahbd', p['cmp_w2'], eye).reshape(
        2, HEADS_PER_SLAB * CMP_HIDDEN, LANES).astype(BF16)
    return w


def _split3(m):
    return m.reshape(m.shape[0], 3, D_MODEL).transpose(1, 0, 2)


def _ffn_call(x, mod, layer, p, w, prefix):
    bsz = x.shape[0]
    halo = (CONV_WIDTH - 1) * bsz
    pf = prefix.transpose(1, 0, 2).reshape(halo, 2 * D_FF)
    cw, cb = p['ffn_conv_w'][layer], p['ffn_conv_b'][layer].reshape(1, -1)
    y, sta, stg = _ffn_layer(x, mod, p['norm_pre'][layer, 1:2], p['norm_post'][layer, 1:2],
                             w['ffn_wa'][layer], w['ffn_wg'][layer], cw[:, :D_FF], cw[:, D_FF:],
                             cb[:, :D_FF], cb[:, D_FF:], w['ffn_wd'][layer], pf[:, :D_FF], pf[:, D_FF:],
                             FFN_T, FFN_TN)
    st = jnp.stack([sta, stg]).reshape(2, D_FF // FFN_TN, CONV_WIDTH - 1, bsz, FFN_TN)
    st = st.transpose(3, 2, 0, 1, 4).reshape(bsz, CONV_WIDTH - 1, 2 * D_FF)
    return y, st


def _layer0(x, mods, p, w, h0, prefix):
    bsz = x.shape[0]
    y, s = _s5_layer(x, _split3(mods[0]), p['norm_pre'][0, 0:1], p['norm_post'][0, 0:1], w['s5_wb'],
                     w['s5_wcre'], w['s5_wcim'], w['s5_a'], p['ssm_d'][0:1], w['w_glu'], p['b_glu'][0:1],
                     h0, S5_T)
    s = s.reshape(2, bsz, N_GROUPS, SSM_STATE)
    y, st = _ffn_call(y, _split3(mods[1]), 0, p, w, prefix)
    return y, s[0], s[1], st


PROJ_T = 512


def _proj_call(x, mods, modkv, p, w, pos, bt):
    bsz, seq, _ = x.shape
    cos, sin = _rope_tables(pos)
    if bt > 1:
        cos, sin = jnp.tile(cos, (bt, 1)), jnp.tile(sin, (bt, 1))
    mkv = modkv.reshape(bsz, 2, 1, D_MODEL).transpose(1, 0, 2, 3)
    mq = _split3(mods[2]).reshape(3, bsz, 1, D_MODEL)
    return _proj(x, mkv, mq, p['norm_kv'].reshape(1, D_MODEL), p['norm_pre'][1, 0:1], w['w_kv'], w['w_qg'],
                 cos, sin, bt, PROJ_T, bt == 1)


def _prompt_group(x, mods, modkv, p, w):
    bsz, seq, _ = x.shape
    zeros_state = jnp.zeros((2, bsz, N_STATE), F32)
    zeros_prefix = jnp.zeros((bsz, CONV_WIDTH - 1, 2 * D_FF), F32)
    x1, s_re, s_im, conv0 = _layer0(x, mods, p, w, zeros_state, zeros_prefix)
    rows, win, kk, qpt, qrt, gatet, vt = _proj_call(x1, mods, modkv, p, w, np.arange(seq), 1)
    n_pages = seq // PAGE_SIZE
    table = jnp.arange(bsz * n_pages, dtype=jnp.int32)
    cmp, cmpt = _compress(rows.reshape(bsz * n_pages, PAGE_SIZE, 4 * KV_WIDTH), table, n_pages,
                          w['cmp_pe'], w['cmp_w1'], w['cmp_w1bd'], w['cmp_b1'], w['cmp_w2'], False)
    gate_mod = _split3(mods[2])[2].reshape(bsz, 1, D_MODEL)
    x2 = _nsa_prompt(x1, qpt, qrt, gatet, cmp, cmpt, kk, vt, gate_mod, p['norm_post'][1, 0:1], w['w_o_t'],
                     NSA_TQ, NSA_TK)
    y, conv1 = _ffn_call(x2, _split3(mods[3]), 1, p, w, zeros_prefix)
    kvd = (N_KV_HEADS, HEAD_DIM)
    return (y, rows.reshape(bsz, seq, 4, *kvd), win.reshape(bsz, -1, 2, *kvd),
            s_re[None], s_im[None], jnp.stack([conv0, conv1]))


def _block_diag_rows(q):
    bsz, steps, _ = q.shape
    q5 = q.reshape(bsz, steps, N_KV_HEADS, GQA_GROUP, HEAD_DIM)
    eye = jnp.eye(N_KV_HEADS, dtype=q.dtype)
    return jnp.einsum('bqkgd,kj->bgkqjd', q5, eye).reshape(bsz, GQA_GROUP * N_KV_HEADS * steps, KV_WIDTH)


def _sample_group(x, mods, modkv, p, w, pool, page_table, wcache, ssm_re, ssm_im, conv_state):
    bsz, steps, _ = x.shape
    n_pages = page_table.shape[1]
    past_len = n_pages * PAGE_SIZE
    h0 = jnp.stack([ssm_re[0].reshape(bsz, N_STATE), ssm_im[0].reshape(bsz, N_STATE)])
    x1, s_re, s_im, conv0 = _layer0(x, mods, p, w, h0, conv_state[0])
    rows, win, kvb, qp, qr, gates = _proj_call(x1, mods, modkv, p, w, past_len + np.arange(steps), bsz)
    pool2 = pool.transpose(0, 2, 3, 4, 1).reshape(pool.shape[0], 4 * KV_WIDTH, PAGE_SIZE)
    wcache2 = wcache.transpose(0, 2, 3, 4, 1).reshape(bsz, 2 * KV_WIDTH, wcache.shape[1])
    table = page_table.reshape(-1)
    cmp, _ = _compress(pool2, table, n_pages, w['cmp_pe'], w['cmp_w1'], w['cmp_w1bd'], w['cmp_b1'], w['cmp_w2'],
                       True)
    nrow = GQA_GROUP * N_KV_HEADS * steps
    g5 = gates[:, :, :3 * N_HEADS].reshape(bsz, steps, 3, N_KV_HEADS, GQA_GROUP)
    gates_r = jnp.pad(g5.transpose(0, 4, 3, 1, 2).reshape(bsz, nrow, 3), ((0, 0), (0, 0), (0, LANES - 3)))
    o_bd = _nsa_sample(pool2, table, n_pages, _block_diag_rows(qp), _block_diag_rows(qr), gates_r, cmp, kvb,
                       wcache2, past_len)
    o6 = o_bd.reshape(bsz, GQA_GROUP, N_KV_HEADS, steps, N_KV_HEADS, HEAD_DIM)
    o = jnp.einsum('bgkqjd,kj->bqkgd', o6, jnp.eye(N_KV_HEADS, dtype=F32)).reshape(bsz * steps, D_MODEL)
    gate_mod = jnp.repeat(_split3(mods[2])[2], steps, axis=0)
    x2 = _oproj(o, x1.reshape(bsz * steps, D_MODEL), gate_mod, p['norm_post'][1, 0:1], w['w_o'])
    y, conv1 = _ffn_call(x2.reshape(bsz, steps, D_MODEL), _split3(mods[3]), 1, p, w, conv_state[1])
    kvd = (N_KV_HEADS, HEAD_DIM)
    return (y, rows.reshape(bsz, steps, 4, *kvd), win.reshape(bsz, steps, 2, *kvd),
            s_re[None], s_im[None], jnp.stack([conv0, conv1]))


def kernel(x_prompt, x_sample, cache_nsa_paged, cache_nsa_window, state_ssm_re, state_ssm_im, state_ffn_conv,
           page_table, c_prompt, c_sample, norm_pre, norm_post, w_mod, b_mod, ssm_lam_re, ssm_lam_im,
           ssm_log_dt, ssm_b_re, ssm_b_im, ssm_c_re, ssm_c_im, ssm_d, w_glu, b_glu, ffn_w_up, ffn_conv_w,
           ffn_conv_b, ffn_w_down, norm_kv, w_mod_kv, b_mod_kv, w_kv, cmp_pe, cmp_w1, cmp_b1, cmp_w2,
           w_qg, w_o):
    p = {'norm_pre': norm_pre, 'norm_post': norm_post, 'w_mod': w_mod, 'b_mod': b_mod,
         'ssm_lam_re': ssm_lam_re, 'ssm_lam_im': ssm_lam_im, 'ssm_log_dt': ssm_log_dt,
         'ssm_b_re': ssm_b_re, 'ssm_b_im': ssm_b_im, 'ssm_c_re': ssm_c_re, 'ssm_c_im': ssm_c_im,
         'ssm_d': ssm_d, 'w_glu': w_glu, 'b_glu': b_glu, 'ffn_w_up': ffn_w_up, 'ffn_conv_w': ffn_conv_w,
         'ffn_conv_b': ffn_conv_b, 'ffn_w_down': ffn_w_down, 'norm_kv': norm_kv, 'w_mod_kv': w_mod_kv,
         'b_mod_kv': b_mod_kv, 'w_kv': w_kv, 'cmp_pe': cmp_pe, 'cmp_w1': cmp_w1, 'cmp_b1': cmp_b1,
         'cmp_w2': cmp_w2, 'w_qg': w_qg, 'w_o': w_o}
    w = _prepare(p)
    nb, ns = x_prompt.shape[0], x_sample.shape[0]
    c_all = jnp.concatenate([c_prompt, c_sample])
    mods = _mods(c_all, w_mod.reshape(4, D_MODEL, 3 * D_MODEL), b_mod.reshape(4, 3 * D_MODEL))
    modkv = _mods(c_all, w_mod_kv[None], b_mod_kv[None])[0]
    y_p, rows_p, win_p, sre_p, sim_p, conv_p = _prompt_group(x_prompt, mods[:, :nb], modkv[:nb], p, w)
    y_s, rows_s, win_s, sre_s, sim_s, conv_s = _sample_group(
        x_sample, mods[:, nb:], modkv[nb:], p, w, cache_nsa_paged, page_table, cache_nsa_window,
        state_ssm_re, state_ssm_im, state_ffn_conv)
    return (y_p, y_s, rows_p, rows_s, win_p, win_s, sre_p, sim_p, sre_s, sim_s, conv_p, conv_s)
```
